```python
import jax, jax.numpy as jnp
from jax import lax
import numpy as np

D_MODEL = 1024
BATCH = 16
SEQ = 2048
DEPTH = 1

D_MIX = D_MODEL
D_MLSTM = D_MIX // 2
D_RWKV = D_MIX - D_MLSTM
MLSTM_HEADS = 4
MLSTM_HEAD_DIM = D_MLSTM // MLSTM_HEADS
MLSTM_CHUNK = 128
CONV_WIDTH = 4
RWKV_HEAD_DIM = 64
RWKV_HEADS = D_RWKV // RWKV_HEAD_DIM
DECAY_LORA = 64
ICLR_LORA = 64
GATE_LORA = 128
D_IN_MLSTM = 4 * D_MLSTM + 2 * MLSTM_HEADS
D_IN_RWKV = 3 * D_RWKV + DECAY_LORA + ICLR_LORA + GATE_LORA
D_IN = D_IN_MLSTM + D_IN_RWKV
D_FF = -(-8 * D_MODEL // (3 * 256)) * 256
MEM_TOKENS = 256
XATTN_HEADS = 4
XATTN_HEAD_DIM = D_MODEL // XATTN_HEADS
EPS = 1e-6
RWKV_LN_EPS = 64e-5

kernel_name = 'hybrid_mlstm_rwkv7_memxattn_block'


def rms_norm(x, g):
    xf = x.astype(jnp.float32)
    y = xf * lax.rsqrt(jnp.mean(xf * xf, -1, keepdims=True) + EPS)
    return (y * g.astype(jnp.float32)).astype(x.dtype)


def split_cols(z, sizes):
    outs, start = [], 0
    for s in sizes:
        outs.append(z[..., start:start + s])
        start += s
    return outs


def token_shift(z):
    return jnp.pad(z, ((0, 0), (1, 0), (0, 0)))[:, :-1]


def causal_conv(z, w):
    T = z.shape[1]
    zp = jnp.pad(z, ((0, 0), (CONV_WIDTH - 1, 0), (0, 0)))
    out = zp[:, 0:T] * w[0]
    for j in range(1, CONV_WIDTH):
        out = out + zp[:, j:j + T] * w[j]
    return out


def mlstm_chunkwise(q, k, v, i_pre, f_pre):
    B, T, H, DH = q.shape
    L = MLSTM_CHUNK
    NC = T // L
    f32 = jnp.float32
    to_chunks = lambda t: t.astype(f32).reshape(B, NC, L, H, DH).transpose(0, 3, 1, 2, 4)
    q = to_chunks(q)
    k = to_chunks(k) * (DH ** -0.5)
    v = to_chunks(v)
    ig = i_pre.astype(f32).reshape(B, NC, L, H).transpose(0, 3, 1, 2)
    lf = jax.nn.log_sigmoid(f_pre.astype(f32)).reshape(B, NC, L, H).transpose(0, 3, 1, 2)
    b = jnp.cumsum(lf, axis=-1)
    g = b[..., -1]

    a = g[..., None] - b + ig
    m_loc = jnp.max(a, -1)
    wgt = jnp.exp(a - m_loc[..., None])
    dC = jnp.einsum('bhcld,bhcle->bhcde', v * wgt[..., None], k)
    dn = jnp.einsum('bhcl,bhcle->bhce', wgt, k)

    def step(carry, inp):
        C, n, m = carry
        g_c, mloc_c, dC_c, dn_c = inp
        m_new = jnp.maximum(g_c + m, mloc_c)
        s_old = jnp.exp(g_c + m - m_new)
        s_new = jnp.exp(mloc_c - m_new)
        C_new = s_old[..., None, None] * C + s_new[..., None, None] * dC_c
        n_new = s_old[..., None] * n + s_new[..., None] * dn_c
        return (C_new, n_new, m_new), (C, n, m)

    init = (jnp.zeros((B, H, DH, DH), f32), jnp.zeros((B, H, DH), f32), jnp.zeros((B, H), f32))
    xs = (jnp.moveaxis(g, 2, 0), jnp.moveaxis(m_loc, 2, 0), jnp.moveaxis(dC, 2, 0), jnp.moveaxis(dn, 2, 0))
    _, (C_prev, n_prev, m_prev) = lax.scan(step, init, xs)
    C_prev = jnp.moveaxis(C_prev, 0, 2)
    n_prev = jnp.moveaxis(n_prev, 0, 2)
    m_prev = jnp.moveaxis(m_prev, 0, 2)

    causal = jnp.tril(jnp.ones((L, L), dtype=bool))
    Dm = b[..., :, None] - b[..., None, :] + ig[..., None, :]
    Dm = jnp.where(causal, Dm, -jnp.inf)
    m_intra = jnp.max(Dm, -1)
    m_inter = b + m_prev[..., None]
    m_t = jnp.maximum(m_inter, m_intra)
    S = jnp.einsum('bhcld,bhcsd->bhcls', q, k) * jnp.exp(Dm - m_t[..., None])
    s_inter = jnp.exp(m_inter - m_t)
    num = jnp.einsum('bhcls,bhcsd->bhcld', S, v) + s_inter[..., None] * jnp.einsum('bhcde,bhcle->bhcld', C_prev, q)
    den = jnp.sum(S, -1) + s_inter * jnp.einsum('bhce,bhcle->bhcl', n_prev, q)
    h = num / jnp.maximum(jnp.abs(den), jnp.exp(-m_t))[..., None]
    return h.transpose(0, 2, 3, 1, 4).reshape(B, T, H, DH)


def mlstm_group(z, conv_w, i_bias, f_bias, norm_w):
    B, T, _ = z.shape
    qk, v, o, ig, fg = split_cols(z, (2 * D_MLSTM, D_MLSTM, D_MLSTM, MLSTM_HEADS, MLSTM_HEADS))
    qk = jax.nn.silu(causal_conv(qk, conv_w))
    q, k = qk[..., :D_MLSTM], qk[..., D_MLSTM:]
    shp = (B, T, MLSTM_HEADS, MLSTM_HEAD_DIM)
    h = mlstm_chunkwise(q.reshape(shp), k.reshape(shp), v.reshape(shp), ig + i_bias, fg + f_bias)
    h = h * lax.rsqrt(jnp.mean(h * h, -1, keepdims=True) + EPS)
    h = h * norm_w.astype(jnp.float32).reshape(MLSTM_HEADS, MLSTM_HEAD_DIM)
    return (h.reshape(B, T, D_MLSTM) * jax.nn.sigmoid(o.astype(jnp.float32))).astype(z.dtype)


def rwkv7_scan(r, w, k, v, a, b):
    B, T, H, N = r.shape

    def step(S, inp):
        r_t, w_t, k_t, v_t, a_t, b_t = inp
        sa = jnp.einsum('bhij,bhj->bhi', S, a_t)
        S = S * w_t[:, :, None, :] + sa[..., None] * b_t[:, :, None, :] + v_t[..., None] * k_t[:, :, None, :]
        return S, jnp.einsum('bhij,bhj->bhi', S, r_t)

    xs = tuple(jnp.moveaxis(t, 1, 0) for t in (r, w, k, v, a, b))
    _, y = lax.scan(step, jnp.zeros((B, H, N, N), jnp.float32), xs)
    return jnp.moveaxis(y, 0, 1)


def rwkv7_group(z, mu, w0, w_up, a0, a_up, g_up, k_k, k_a, r_k, ln_w, ln_b):
    B, T, _ = z.shape
    f32 = jnp.float32
    z = z + (token_shift(z) - z) * mu
    r, k, v, xw, xa, xg = split_cols(z, (D_RWKV, D_RWKV, D_RWKV, DECAY_LORA, ICLR_LORA, GATE_LORA))
    w = -jax.nn.softplus(-(w0 + jnp.tanh(xw) @ w_up).astype(f32)) - 0.5
    decay = jnp.exp(-jnp.exp(w))
    a = jax.nn.sigmoid((a0 + xa @ a_up).astype(f32))
    g = jax.nn.sigmoid(xg) @ g_up
    hs = lambda t: t.astype(f32).reshape(B, T, RWKV_HEADS, RWKV_HEAD_DIM)
    hp = lambda t: t.astype(f32).reshape(RWKV_HEADS, RWKV_HEAD_DIM)
    kk = hs(k * k_k)
    kk = kk / jnp.maximum(jnp.sqrt(jnp.sum(kk * kk, -1, keepdims=True)), 1e-12)
    a = hs(a)
    k = hs(k) * (1.0 + (a - 1.0) * hp(k_a))
    r, v = hs(r), hs(v)
    y = rwkv7_scan(r, hs(decay), k, v, -kk, kk * a)
    mean = jnp.mean(y, -1, keepdims=True)
    var = jnp.mean(jnp.square(y - mean), -1, keepdims=True)
    y = (y - mean) * lax.rsqrt(var + RWKV_LN_EPS)
    y = y * hp(ln_w) + hp(ln_b)
    y = y + jnp.sum(r * k * r_k.astype(f32), -1, keepdims=True) * v
    return (y.reshape(B, T, D_RWKV) * g.astype(f32)).astype(z.dtype)


def memory_cross_attention(u, m, wq, wkv, wo):
    B, T, D = u.shape
    M = m.shape[1]
    q = (u @ wq).reshape(B, T, XATTN_HEADS, XATTN_HEAD_DIM)
    kv = m @ wkv
    k = kv[..., :D].reshape(B, M, XATTN_HEADS, XATTN_HEAD_DIM)
    v = kv[..., D:].reshape(B, M, XATTN_HEADS, XATTN_HEAD_DIM)
    s = jnp.einsum('bthd,bmhd->bhtm', q, k).astype(jnp.float32) * (XATTN_HEAD_DIM ** -0.5)
    p = jax.nn.softmax(s, -1).astype(v.dtype)
    o = jnp.einsum('bhtm,bmhd->bthd', p, v).reshape(B, T, D)
    return o @ wo


def setup_inputs(seed: int = 0) -> dict:
    key = jax.random.key(seed)
    ks = jax.random.split(key, 32)
    f32 = jnp.float32
    nrm = lambda k, shape, s: jax.random.normal(k, shape, f32) * s
    L_ = DEPTH
    ramp_w0 = jnp.linspace(-6.0, -1.0, D_RWKV, dtype=f32)
    ramp_fb = jnp.linspace(3.0, 6.0, MLSTM_HEADS, dtype=f32)
    return {
        'x': nrm(ks[0], (BATCH, SEQ, D_MODEL), 1.0),
        'mem': nrm(ks[1], (BATCH, MEM_TOKENS, D_MODEL), 1.0),
        'norm_mix': 1.0 + nrm(ks[2], (L_, D_MODEL), 0.02),
        'w_in': nrm(ks[3], (L_, D_MODEL, D_IN), D_MODEL ** -0.5),
        'mlstm_conv': nrm(ks[4], (L_, CONV_WIDTH, 2 * D_MLSTM), CONV_WIDTH ** -0.5),
        'mlstm_i_bias': nrm(ks[5], (L_, MLSTM_HEADS), 0.1),
        'mlstm_f_bias': ramp_fb + nrm(ks[6], (L_, MLSTM_HEADS), 0.1),
        'mlstm_norm': 1.0 + nrm(ks[7], (L_, D_MLSTM), 0.02),
        'rwkv_mu': jax.random.uniform(ks[8], (L_, D_IN_RWKV), f32),
        'rwkv_w0': ramp_w0 + nrm(ks[9], (L_, D_RWKV), 0.1),
        'rwkv_w_up': nrm(ks[10], (L_, DECAY_LORA, D_RWKV), 0.3 * DECAY_LORA ** -0.5),
        'rwkv_a0': nrm(ks[11], (L_, D_RWKV), 0.1),
        'rwkv_a_up': nrm(ks[12], (L_, ICLR_LORA, D_RWKV), ICLR_LORA ** -0.5),
        'rwkv_g_up': nrm(ks[13], (L_, GATE_LORA, D_RWKV), GATE_LORA ** -0.5),
        'rwkv_k_k': 0.85 + nrm(ks[14], (L_, D_RWKV), 0.05),
        'rwkv_k_a': 1.0 + nrm(ks[15], (L_, D_RWKV), 0.05),
        'rwkv_r_k': nrm(ks[16], (L_, RWKV_HEADS, RWKV_HEAD_DIM), 0.1),
        'rwkv_ln_w': 1.0 + nrm(ks[17], (L_, D_RWKV), 0.02),
        'rwkv_ln_b': nrm(ks[18], (L_, D_RWKV), 0.02),
        'w_mix_out': nrm(ks[19], (L_, D_MIX, D_MODEL), D_MIX ** -0.5),
        'norm_xattn': 1.0 + nrm(ks[20], (L_, D_MODEL), 0.02),
        'norm_mem': 1.0 + nrm(ks[21], (L_, D_MODEL), 0.02),
        'xattn_wq': nrm(ks[22], (L_, D_MODEL, D_MODEL), D_MODEL ** -0.5),
        'xattn_wkv': nrm(ks[23], (L_, D_MODEL, 2 * D_MODEL), D_MODEL ** -0.5),
        'xattn_wo': nrm(ks[24], (L_, D_MODEL, D_MODEL), D_MODEL ** -0.5),
        'norm_ffn': 1.0 + nrm(ks[25], (L_, D_MODEL), 0.02),
        'ffn_w_gate': nrm(ks[26], (L_, D_MODEL, D_FF), D_MODEL ** -0.5),
        'ffn_w_up': nrm(ks[27], (L_, D_MODEL, D_FF), D_MODEL ** -0.5),
        'ffn_w_down': nrm(ks[28], (L_, D_FF, D_MODEL), D_FF ** -0.5),
        'norm_final': 1.0 + nrm(ks[29], (D_MODEL,), 0.02),
    }


def reference(x, mem, norm_mix, w_in, mlstm_conv, mlstm_i_bias, mlstm_f_bias, mlstm_norm, rwkv_mu, rwkv_w0, rwkv_w_up, rwkv_a0, rwkv_a_up, rwkv_g_up, rwkv_k_k, rwkv_k_a, rwkv_r_k, rwkv_ln_w, rwkv_ln_b, w_mix_out, norm_xattn, norm_mem, xattn_wq, xattn_wkv, xattn_wo, norm_ffn, ffn_w_gate, ffn_w_up, ffn_w_down, norm_final):
    h = x
    for l in range(DEPTH):
        u = rms_norm(h, norm_mix[l])
        z = u @ w_in[l]
        y_m = mlstm_group(z[..., :D_IN_MLSTM], mlstm_conv[l], mlstm_i_bias[l], mlstm_f_bias[l], mlstm_norm[l])
        y_r = rwkv7_group(z[..., D_IN_MLSTM:], rwkv_mu[l], rwkv_w0[l], rwkv_w_up[l], rwkv_a0[l], rwkv_a_up[l],
                          rwkv_g_up[l], rwkv_k_k[l], rwkv_k_a[l], rwkv_r_k[l], rwkv_ln_w[l], rwkv_ln_b[l])
        h = h + jnp.concatenate([y_m, y_r], axis=-1) @ w_mix_out[l]
        h = h + memory_cross_attention(rms_norm(h, norm_xattn[l]), rms_norm(mem, norm_mem[l]),
                                       xattn_wq[l], xattn_wkv[l], xattn_wo[l])
        u = rms_norm(h, norm_ffn[l])
        h = h + (jax.nn.silu(u @ ffn_w_gate[l]) * (u @ ffn_w_up[l])) @ ffn_w_down[l]
    return rms_norm(h, norm_final)
```

```python
import functools
import math

import jax
import jax.numpy as jnp
from jax import lax
from jax.experimental import pallas as pl
from jax.experimental.pallas import tpu as pltpu

F32 = jnp.float32
BF16 = jnp.bfloat16

EPS = 1e-6
RWKV_LN_EPS = 64e-5
CONV_WIDTH = 4

D_MODEL = 1024
D_MLSTM = 512
D_RWKV = 512
MLSTM_HEADS = 4
MLSTM_HEAD_DIM = 128
RWKV_HEAD_DIM = 64
RWKV_PAIRS = D_RWKV // (2 * RWKV_HEAD_DIM)
DECAY_LORA = 64
ICLR_LORA = 64
GATE_LORA = 128
XATTN_HEADS = 4
XATTN_HEAD_DIM = 256
CHUNK = 128
INV_BASE = 16

C_QK = 0
C_MV = C_QK + 2 * D_MLSTM
C_MO = C_MV + D_MLSTM
C_MG = C_MO + D_MLSTM
C_RW = C_MG + 128
D_RW_IN = 3 * D_RWKV + DECAY_LORA + ICLR_LORA + GATE_LORA
C_END = C_RW + D_RW_IN

VMEM_LIMIT = 56 * 1024 * 1024


def _bdot(a, b):
    return jnp.dot(a.astype(BF16), b.astype(BF16), preferred_element_type=F32)


def _bdot_nt(a, b):
    return lax.dot_general(a.astype(BF16), b.astype(BF16), (((1,), (1,)), ((), ())),
                           preferred_element_type=F32)


def _bdot_tn(a, b):
    return lax.dot_general(a.astype(BF16), b.astype(BF16), (((0,), (0,)), ((), ())),
                           preferred_element_type=F32)


def _split3(x):
    hi = x.astype(BF16)
    r1 = x - hi.astype(F32)
    mid = r1.astype(BF16)
    lo = (r1 - mid.astype(F32)).astype(BF16)
    return hi, mid, lo


def _exact_left_dot(ones_mat, x):
    hi, mid, lo = _split3(x)
    acc = jnp.dot(ones_mat, hi, preferred_element_type=F32)
    acc += jnp.dot(ones_mat, mid, preferred_element_type=F32)
    acc += jnp.dot(ones_mat, lo, preferred_element_type=F32)
    return acc


def _exact_right_dot(x, ones_mat):
    hi = x.astype(BF16)
    lo = (x - hi.astype(F32)).astype(BF16)
    return (jnp.dot(hi, ones_mat, preferred_element_type=F32)
            + jnp.dot(lo, ones_mat, preferred_element_type=F32))


def _rms(x, g):
    return x * lax.rsqrt(jnp.mean(x * x, axis=-1, keepdims=True) + EPS) * g


def _sigmoid(x):
    return 1.0 / (1.0 + jnp.exp(-x))


def _log_sigmoid(x):
    return jnp.minimum(x, 0.0) - jnp.log1p(jnp.exp(-jnp.abs(x)))


def _softplus(x):
    return jnp.maximum(x, 0.0) + jnp.log1p(jnp.exp(-jnp.abs(x)))


def _in_proj_kernel(x_ref, g_ref, w_ref, z_ref):
    u = _rms(x_ref[...], g_ref[...]).astype(BF16)
    z_ref[...] = jnp.dot(u, w_ref[...], preferred_element_type=F32)


def _in_proj(x2, g, w, tm):
    n, d = x2.shape
    c = w.shape[1]
    return pl.pallas_call(
        _in_proj_kernel,
        grid=(n // tm,),
        in_specs=[pl.BlockSpec((tm, d), lambda i: (i, 0)),
                  pl.BlockSpec((1, d), lambda i: (0, 0)),
                  pl.BlockSpec((d, c), lambda i: (0, 0), pipeline_mode=pl.Buffered(1))],
        out_specs=pl.BlockSpec((tm, c), lambda i: (i, 0)),
        out_shape=jax.ShapeDtypeStruct((n, c), F32),
        compiler_params=pltpu.CompilerParams(dimension_semantics=("parallel",),
                                             vmem_limit_bytes=VMEM_LIMIT),
        name="in_proj",
    )(x2, g, w)


def _mem_kv_kernel(m_ref, g_ref, w_ref, kv_ref):
    u = _rms(m_ref[0], g_ref[...]).astype(BF16)
    kv_ref[0] = jnp.dot(u, w_ref[...], preferred_element_type=F32).astype(BF16)


def _mem_kv(mem, g, w):
    b, m, d = mem.shape
    c = w.shape[1]
    return pl.pallas_call(
        _mem_kv_kernel,
        grid=(b,),
        in_specs=[pl.BlockSpec((1, m, d), lambda i: (i, 0, 0)),
                  pl.BlockSpec((1, d), lambda i: (0, 0)),
                  pl.BlockSpec((d, c), lambda i: (0, 0), pipeline_mode=pl.Buffered(1))],
        out_specs=pl.BlockSpec((1, m, c), lambda i: (i, 0, 0)),
        out_shape=jax.ShapeDtypeStruct((b, m, c), BF16),
        compiler_params=pltpu.CompilerParams(dimension_semantics=("parallel",),
                                             vmem_limit_bytes=VMEM_LIMIT),
        name="mem_kv",
    )(mem, g, w)


def _unit_lower_inverse(m_strict, row, col):
    n = m_strict.shape[0]
    eye = (row == col).astype(F32)

    def same_block(s):
        return (row // s) == (col // s)

    p = jnp.where(same_block(INV_BASE), m_strict, 0.0)
    t = eye + p
    s = 1
    while 2 * s < INV_BASE:
        p = _bdot(p, p)
        t = t + _bdot(t, p)
        s *= 2
    s = INV_BASE
    while s < n:
        c = jnp.where(same_block(2 * s) & jnp.logical_not(same_block(s)), m_strict, 0.0)
        t = t + _bdot(t, _bdot(c, t))
        s *= 2
    return t


def _mixer_kernel(z_ref, conv_ref, gbias_ref, mnorm_ref, mu_ref, w0_ref, a0_ref, wa_ref, gup_ref,
                  kk_ref, ka_ref, rk_ref, lnw_ref, lnb_ref, seg_ref,
                  y_ref,
                  qkbuf, rwbuf, ct_ref, n_ref, m_ref, s_ref):
    L = CHUNK
    t_idx = pl.program_id(1)

    @pl.when(t_idx == 0)
    def _():
        qkbuf[...] = jnp.zeros_like(qkbuf)
        rwbuf[...] = jnp.zeros_like(rwbuf)
        ct_ref[...] = jnp.zeros_like(ct_ref)
        n_ref[...] = jnp.zeros_like(n_ref)
        m_ref[...] = jnp.zeros_like(m_ref)
        s_ref[...] = jnp.zeros_like(s_ref)

    @pl.when(t_idx > 0)
    def _():
        qkbuf[0:8, :] = qkbuf[L:L + 8, :]
        rwbuf[0:8, :] = rwbuf[L:L + 8, :]

    qkbuf[8:8 + L, :] = z_ref[0, :, C_QK:C_QK + 2 * D_MLSTM]
    rwbuf[8:8 + L, :] = z_ref[0, :, C_RW:C_END]

    row = lax.broadcasted_iota(jnp.int32, (L, L), 0)
    col = lax.broadcasted_iota(jnp.int32, (L, L), 1)
    lower = row >= col
    strict = row > col
    tri = lower.astype(BF16)

    conv_w = conv_ref[...]
    qk = qkbuf[8:8 + L, :] * conv_w[CONV_WIDTH - 1:CONV_WIDTH, :]
    for j in range(1, CONV_WIDTH):
        qk += qkbuf[8 - j:8 - j + L, :] * conv_w[CONV_WIDTH - 1 - j:CONV_WIDTH - j, :]
    qk = qk * _sigmoid(qk)
    q_all = qk[:, :D_MLSTM]
    k_all = qk[:, D_MLSTM:] * (MLSTM_HEAD_DIM ** -0.5)

    gates = z_ref[0, :, C_MG:C_MG + 128] + gbias_ref[...]
    bcum = _exact_left_dot(tri, _log_sigmoid(gates))
    gates_t = gates.T
    bcum_t = bcum.T

    for h in range(MLSTM_HEADS):
        hs = slice(h * MLSTM_HEAD_DIM, (h + 1) * MLSTM_HEAD_DIM)
        qh = q_all[:, hs]
        kh = k_all[:, hs]
        vh = z_ref[0, :, C_MV + h * MLSTM_HEAD_DIM:C_MV + (h + 1) * MLSTM_HEAD_DIM]
        oh = z_ref[0, :, C_MO + h * MLSTM_HEAD_DIM:C_MO + (h + 1) * MLSTM_HEAD_DIM]
        f_lane = MLSTM_HEADS + h
        b_col = bcum[:, f_lane:f_lane + 1]
        b_row = bcum_t[f_lane:f_lane + 1, :]
        ig_col = gates[:, h:h + 1]
        ig_row = gates_t[h:h + 1, :]
        g_tot = bcum[L - 1:L, f_lane:f_lane + 1]
        m_prev = m_ref[h][:, 0:1]
        ct_prev = ct_ref[h]
        n_prev = n_ref[h]

        dm = jnp.where(lower, b_col - b_row + ig_row, -jnp.inf)
        m_intra = jnp.max(dm, axis=-1, keepdims=True)
        m_inter = b_col + m_prev
        m_t = jnp.maximum(m_inter, m_intra)
        s_mat = _bdot_nt(qh, kh) * jnp.exp(dm - m_t)
        s_inter = jnp.exp(m_inter - m_t)
        num = _bdot(s_mat, vh) + s_inter * _bdot(qh, ct_prev)
        den = (jnp.sum(s_mat, axis=-1, keepdims=True)
               + s_inter * jnp.sum(qh * n_prev, axis=-1, keepdims=True))
        hh = num / jnp.maximum(jnp.abs(den), jnp.exp(-m_t))

        a_col = g_tot - b_col + ig_col
        m_loc = jnp.max(a_col, axis=0, keepdims=True)
        wgt = jnp.exp(a_col - m_loc)
        d_ct = _bdot_tn(kh, vh * wgt)
        d_n = jnp.sum(kh * wgt, axis=0, keepdims=True)
        m_new = jnp.maximum(g_tot + m_prev, m_loc)
        s_old = jnp.exp(g_tot + m_prev - m_new)
        s_new = jnp.exp(m_loc - m_new)
        ct_ref[h] = s_old * ct_prev + s_new * d_ct
        n_ref[h] = s_old * n_prev + s_new * d_n
        m_ref[h] = jnp.broadcast_to(m_new, (1, 128))

        hh = hh * lax.rsqrt(jnp.mean(hh * hh, axis=-1, keepdims=True) + EPS)
        hh = hh * mnorm_ref[:, hs] * _sigmoid(oh)
        y_ref[0, :, hs] = hh.astype(y_ref.dtype)

    z_now = rwbuf[8:8 + L, :]
    z_prev = rwbuf[7:7 + L, :]
    zm = z_now + (z_prev - z_now) * mu_ref[...]
    r = zm[:, 0:D_RWKV]
    k = zm[:, D_RWKV:2 * D_RWKV]
    v = zm[:, 2 * D_RWKV:3 * D_RWKV]
    xwa = zm[:, 3 * D_RWKV:3 * D_RWKV + 128]
    xg = zm[:, 3 * D_RWKV + 128:3 * D_RWKV + 256]
    lane128 = lax.broadcasted_iota(jnp.int32, (L, 128), 1)
    lora_in = jnp.where(lane128 < DECAY_LORA, jnp.tanh(xwa), xwa)
    lora = _bdot(lora_in, wa_ref[...])
    w_raw = -_softplus(-(w0_ref[...] + lora[:, :D_RWKV])) - 0.5
    logw = -jnp.exp(w_raw)
    iclr = _sigmoid(a0_ref[...] + lora[:, D_RWKV:])
    gate = _bdot(_sigmoid(xg), gup_ref[...])

    seg = seg_ref[...]
    kk = k * kk_ref[...]
    kk = kk / jnp.maximum(jnp.sqrt(_exact_right_dot(kk * kk, seg)), 1e-12)
    k2 = k * (1.0 + (iclr - 1.0) * ka_ref[...])
    a_vec = -kk
    b_vec = kk * iclr

    cum = _exact_left_dot(tri, logw)
    c_ref = cum[L // 2 - 1:L // 2, :]
    c_end = cum[L - 1:L, :]
    e_pos = jnp.exp(cum - c_ref)
    e_neg = jnp.exp(c_ref - cum)
    e_abs = jnp.exp(cum)
    e_tail = jnp.exp(c_end - cum)
    e_shift = jnp.exp(-logw)
    r_rel = r * e_pos
    a_rel = a_vec * e_pos * e_shift
    k_rel = k2 * e_neg
    b_rel = b_vec * e_neg
    r_abs = r * e_abs
    a_abs = a_vec * e_abs * e_shift
    k_tail = k2 * e_tail
    b_tail = b_vec * e_tail
    e_end = jnp.exp(c_end)

    lane_lo = lane128 < RWKV_HEAD_DIM
    lane_lo2 = lax.broadcasted_iota(jnp.int32, (2 * L, 128), 1) < RWKV_HEAD_DIM
    blockdiag = ((row // RWKV_HEAD_DIM) == (col // RWKV_HEAD_DIM)).astype(F32)

    y_parts = []
    for p in range(RWKV_PAIRS):
        ps = slice(p * 128, (p + 1) * 128)
        v_p = v[:, ps]
        s_prev = s_ref[p]
        ar_rel = jnp.concatenate([a_rel[:, ps], r_rel[:, ps]], axis=0)
        bk_rel = jnp.concatenate([b_rel[:, ps], k_rel[:, ps]], axis=0)
        ar_abs = jnp.concatenate([a_abs[:, ps], r_abs[:, ps]], axis=0)
        inter = _bdot_nt(ar_abs, s_prev)
        mv = []
        t_inv = []
        g_blocks = []
        for lo in (True, False):
            sel = lane_lo2 if lo else jnp.logical_not(lane_lo2)
            g_mat = _bdot_nt(jnp.where(sel, ar_rel, 0.0), bk_rel)
            m_ab = jnp.where(strict, g_mat[:L, :L], 0.0)
            m_ak = jnp.where(strict, g_mat[:L, L:], 0.0)
            m_rb = jnp.where(lower, g_mat[L:, :L], 0.0)
            m_rk = jnp.where(lower, g_mat[L:, L:], 0.0)
            g_blocks.append((m_rb, m_rk))
            mv.append(_bdot(m_ak, v_p))
            t_inv.append(_unit_lower_inverse(m_ab, row, col))
        rhs = inter[:L] + jnp.where(lane_lo, mv[0], mv[1])
        u_p = jnp.where(lane_lo, _bdot(t_inv[0], rhs), _bdot(t_inv[1], rhs))
        uv = jnp.concatenate([u_p, v_p], axis=0)
        y_lo = _bdot(jnp.concatenate(g_blocks[0], axis=1), uv)
        y_hi = _bdot(jnp.concatenate(g_blocks[1], axis=1), uv)
        y_parts.append(inter[L:] + jnp.where(lane_lo, y_lo, y_hi))
        bk_tail = jnp.concatenate([b_tail[:, ps], k_tail[:, ps]], axis=0)
        s_ref[p] = s_prev * e_end[:, ps] + blockdiag * _bdot_tn(uv, bk_tail)

    y = jnp.concatenate(y_parts, axis=1)
    inv_n = 1.0 / RWKV_HEAD_DIM
    mean = _exact_right_dot(y, seg) * inv_n
    yc = y - mean
    var = _exact_right_dot(yc * yc, seg) * inv_n
    y = yc * lax.rsqrt(var + RWKV_LN_EPS) * lnw_ref[...] + lnb_ref[...]
    y = y + _exact_right_dot(r * k2 * rk_ref[...], seg) * v
    y_ref[0, :, D_MLSTM:] = (y * gate).astype(y_ref.dtype)


def _mixer(z, conv_w, gbias, mnorm, mu, w0, a0, wa, gup, kk, ka, rk, lnw, lnb, seg):
    b, t, c = z.shape
    nt = t // CHUNK
    const = lambda a: pl.BlockSpec(a.shape, lambda i, j: (0,) * a.ndim)
    params = (conv_w, gbias, mnorm, mu, w0, a0, wa, gup, kk, ka, rk, lnw, lnb, seg)
    return pl.pallas_call(
        _mixer_kernel,
        grid=(b, nt),
        in_specs=[pl.BlockSpec((1, CHUNK, c), lambda i, j: (i, j, 0))] + [const(a) for a in params],
        out_specs=pl.BlockSpec((1, CHUNK, D_MODEL), lambda i, j: (i, j, 0)),
        out_shape=jax.ShapeDtypeStruct((b, t, D_MODEL), BF16),
        scratch_shapes=[
            pltpu.VMEM((CHUNK + 8, 2 * D_MLSTM), F32),
            pltpu.VMEM((CHUNK + 8, D_RW_IN), F32),
            pltpu.VMEM((MLSTM_HEADS, MLSTM_HEAD_DIM, MLSTM_HEAD_DIM), F32),
            pltpu.VMEM((MLSTM_HEADS, 1, MLSTM_HEAD_DIM), F32),
            pltpu.VMEM((MLSTM_HEADS, 1, 128), F32),
            pltpu.VMEM((RWKV_PAIRS, 128, 128), F32),
        ],
        compiler_params=pltpu.CompilerParams(dimension_semantics=("parallel", "arbitrary"),
                                             vmem_limit_bytes=VMEM_LIMIT),
        name="mixer",
    )(z, *params)


def _tail_kernel(x_ref, y_ref, kv_ref, wmix_ref, gx_ref, wq_ref, wo_ref, gf_ref, wg_ref, wu_ref, wd_ref,
                 gfin_ref, o_ref):
    h = x_ref[...] + jnp.dot(y_ref[...], wmix_ref[...], preferred_element_type=F32)

    q = _bdot(_rms(h, gx_ref[...]), wq_ref[...])
    heads = []
    for i in range(XATTN_HEADS):
        hs = slice(i * XATTN_HEAD_DIM, (i + 1) * XATTN_HEAD_DIM)
        k_h = kv_ref[0, :, i * XATTN_HEAD_DIM:(i + 1) * XATTN_HEAD_DIM]
        v_h = kv_ref[0, :, D_MODEL + i * XATTN_HEAD_DIM:D_MODEL + (i + 1) * XATTN_HEAD_DIM]
        s = _bdot_nt(q[:, hs], k_h) * (XATTN_HEAD_DIM ** -0.5)
        e = jnp.exp(s - jnp.max(s, axis=-1, keepdims=True))
        heads.append(_bdot(e, v_h) / jnp.sum(e, axis=-1, keepdims=True))
    h = h + _bdot(jnp.concatenate(heads, axis=1), wo_ref[...])

    u = _rms(h, gf_ref[...]).astype(BF16)
    gate = jnp.dot(u, wg_ref[...], preferred_element_type=F32)
    up = jnp.dot(u, wu_ref[...], preferred_element_type=F32)
    h = h + _bdot(gate * _sigmoid(gate) * up, wd_ref[...])

    o_ref[...] = _rms(h, gfin_ref[...])


def _tail(x2, y2, kv, wmix, gx, wq, wo, gf, wg, wu, wd, gfin, tm, t):
    n, d = x2.shape
    per_b = t // tm
    const = lambda a: pl.BlockSpec(a.shape, lambda i: (0,) * a.ndim, pipeline_mode=pl.Buffered(1))
    params = (wmix, gx, wq, wo, gf, wg, wu, wd, gfin)
    return pl.pallas_call(
        _tail_kernel,
        grid=(n // tm,),
        in_specs=[pl.BlockSpec((tm, d), lambda i: (i, 0)),
                  pl.BlockSpec((tm, d), lambda i: (i, 0)),
                  pl.BlockSpec((1,) + kv.shape[1:], lambda i: (i // per_b, 0, 0))]
                 + [const(a) for a in params],
        out_specs=pl.BlockSpec((tm, d), lambda i: (i, 0)),
        out_shape=jax.ShapeDtypeStruct((n, d), F32),
        compiler_params=pltpu.CompilerParams(dimension_semantics=("parallel",),
                                             vmem_limit_bytes=VMEM_LIMIT),
        name="tail",
    )(x2, y2, kv, *params)


def _regroup_w_in(w):
    d = w.shape[0]
    n_m = 4 * D_MLSTM
    gates = w[:, n_m:n_m + 2 * MLSTM_HEADS]
    gates = jnp.pad(gates, ((0, 0), (0, 128 - 2 * MLSTM_HEADS)))
    return jnp.concatenate([w[:, :n_m], gates, w[:, n_m + 2 * MLSTM_HEADS:]], axis=1).astype(BF16)


def kernel(x, mem, norm_mix, w_in, mlstm_conv, mlstm_i_bias, mlstm_f_bias, mlstm_norm, rwkv_mu, rwkv_w0, rwkv_w_up, rwkv_a0, rwkv_a_up, rwkv_g_up, rwkv_k_k, rwkv_k_a, rwkv_r_k, rwkv_ln_w, rwkv_ln_b, w_mix_out, norm_xattn, norm_mem, xattn_wq, xattn_wkv, xattn_wo, norm_ffn, ffn_w_gate, ffn_w_up, ffn_w_down, norm_final):
    b, t, d = x.shape
    assert d == D_MODEL and t % CHUNK == 0 and norm_mix.shape[0] == 1
    row = lambda a: a.reshape(1, -1).astype(F32)
    tm = 256 if t % 256 == 0 else CHUNK
    x2 = x.reshape(b * t, d)

    z = _in_proj(x2, row(norm_mix[0]), _regroup_w_in(w_in[0]), tm).reshape(b, t, C_END)
    kv = _mem_kv(mem, row(norm_mem[0]), xattn_wkv[0].astype(BF16))

    gbias = jnp.pad(jnp.concatenate([mlstm_i_bias[0], mlstm_f_bias[0]]), (0, 128 - 2 * MLSTM_HEADS))
    zeros = jnp.zeros((DECAY_LORA, D_RWKV), F32)
    wa = jnp.concatenate([jnp.concatenate([rwkv_w_up[0], zeros], axis=1),
                          jnp.concatenate([zeros, rwkv_a_up[0]], axis=1)], axis=0).astype(BF16)
    head_id = jnp.arange(D_RWKV) // RWKV_HEAD_DIM
    seg = (head_id[:, None] == head_id[None, :]).astype(BF16)
    y = _mixer(z, mlstm_conv[0], row(gbias), row(mlstm_norm[0]), row(rwkv_mu[0]), row(rwkv_w0[0]),
               row(rwkv_a0[0]), wa, rwkv_g_up[0].astype(BF16), row(rwkv_k_k[0]), row(rwkv_k_a[0]),
               row(rwkv_r_k[0]), row(rwkv_ln_w[0]), row(rwkv_ln_b[0]), seg)

    out = _tail(x2, y.reshape(b * t, d), kv, w_mix_out[0].astype(BF16), row(norm_xattn[0]),
                xattn_wq[0].astype(BF16), xattn_wo[0].astype(BF16), row(norm_ffn[0]),
                ffn_w_gate[0].astype(BF16), ffn_w_up[0].astype(BF16), ffn_w_down[0].astype(BF16),
                row(norm_final), tm, t)
    return out.reshape(b, t, d)
```

```python
import functools
import math

import jax
import jax.numpy as jnp
from jax import lax
from jax.experimental import pallas as pl
from jax.experimental.pallas import tpu as pltpu

F32 = jnp.float32
BF16 = jnp.bfloat16

EPS = 1e-6
RWKV_LN_EPS = 64e-5
CONV_WIDTH = 4

D_MODEL = 1024
D_MLSTM = 512
D_RWKV = 512
MLSTM_HEADS = 4
MLSTM_HEAD_DIM = 128
RWKV_HEAD_DIM = 64
RWKV_PAIRS = D_RWKV // (2 * RWKV_HEAD_DIM)
DECAY_LORA = 64
ICLR_LORA = 64
GATE_LORA = 128
XATTN_HEADS = 4
XATTN_HEAD_DIM = 256
CHUNK = 128
INV_BASE = 16

C_QK = 0
C_MV = C_QK + 2 * D_MLSTM
C_MO = C_MV + D_MLSTM
C_MG = C_MO + D_MLSTM
C_RW = C_MG + 128
D_RW_IN = 3 * D_RWKV + DECAY_LORA + ICLR_LORA + GATE_LORA
C_END = C_RW + D_RW_IN

VMEM_LIMIT = 56 * 1024 * 1024


def _bdot(a, b):
    return jnp.dot(a.astype(BF16), b.astype(BF16), preferred_element_type=F32)


def _bdot_nt(a, b):
    return lax.dot_general(a.astype(BF16), b.astype(BF16), (((1,), (1,)), ((), ())),
                           preferred_element_type=F32)


def _bdot_tn(a, b):
    return lax.dot_general(a.astype(BF16), b.astype(BF16), (((0,), (0,)), ((), ())),
                           preferred_element_type=F32)


def _split3(x):
    hi = x.astype(BF16)
    r1 = x - hi.astype(F32)
    mid = r1.astype(BF16)
    lo = (r1 - mid.astype(F32)).astype(BF16)
    return hi, mid, lo


def _exact_left_dot(ones_mat, x):
    hi, mid, lo = _split3(x)
    acc = jnp.dot(ones_mat, hi, preferred_element_type=F32)
    acc += jnp.dot(ones_mat, mid, preferred_element_type=F32)
    acc += jnp.dot(ones_mat, lo, preferred_element_type=F32)
    return acc


def _exact_right_dot(x, ones_mat):
    hi = x.astype(BF16)
    lo = (x - hi.astype(F32)).astype(BF16)
    return (jnp.dot(hi, ones_mat, preferred_element_type=F32)
            + jnp.dot(lo, ones_mat, preferred_element_type=F32))


def _rms(x, g):
    return x * lax.rsqrt(jnp.mean(x * x, axis=-1, keepdims=True) + EPS) * g


def _sigmoid(x):
    return 1.0 / (1.0 + jnp.exp(-x))


def _log_sigmoid(x):
    return jnp.minimum(x, 0.0) - jnp.log1p(jnp.exp(-jnp.abs(x)))


def _softplus(x):
    return jnp.maximum(x, 0.0) + jnp.log1p(jnp.exp(-jnp.abs(x)))


def _in_proj_kernel(x_ref, g_ref, w_ref, z_ref):
    u = _rms(x_ref[...], g_ref[...]).astype(BF16)
    z_ref[...] = jnp.dot(u, w_ref[...], preferred_element_type=F32)


def _in_proj(x2, g, w, tm):
    n, d = x2.shape
    c = w.shape[1]
    return pl.pallas_call(
        _in_proj_kernel,
        grid=(n // tm,),
        in_specs=[pl.BlockSpec((tm, d), lambda i: (i, 0)),
                  pl.BlockSpec((1, d), lambda i: (0, 0)),
                  pl.BlockSpec((d, c), lambda i: (0, 0), pipeline_mode=pl.Buffered(1))],
        out_specs=pl.BlockSpec((tm, c), lambda i: (i, 0)),
        out_shape=jax.ShapeDtypeStruct((n, c), F32),
        compiler_params=pltpu.CompilerParams(dimension_semantics=("parallel",),
                                             vmem_limit_bytes=VMEM_LIMIT),
        name="in_proj",
    )(x2, g, w)


def _mem_kv_kernel(m_ref, g_ref, w_ref, kv_ref):
    u = _rms(m_ref[0], g_ref[...]).astype(BF16)
    kv_ref[0] = jnp.dot(u, w_ref[...], preferred_element_type=F32).astype(BF16)


def _mem_kv(mem, g, w):
    b, m, d = mem.shape
    c = w.shape[1]
    return pl.pallas_call(
        _mem_kv_kernel,
        grid=(b,),
        in_specs=[pl.BlockSpec((1, m, d), lambda i: (i, 0, 0)),
                  pl.BlockSpec((1, d), lambda i: (0, 0)),
                  pl.BlockSpec((d, c), lambda i: (0, 0), pipeline_mode=pl.Buffered(1))],
        out_specs=pl.BlockSpec((1, m, c), lambda i: (i, 0, 0)),
        out_shape=jax.ShapeDtypeStruct((b, m, c), BF16),
        compiler_params=pltpu.CompilerParams(dimension_semantics=("parallel",),
                                             vmem_limit_bytes=VMEM_LIMIT),
        name="mem_kv",
    )(mem, g, w)


def _unit_lower_inverses(ms, row, col):
    n = ms[0].shape[0]
    eye = (row == col).astype(F32)

    def same_block(s):
        return (row // s) == (col // s)

    base = same_block(INV_BASE)
    ps = [jnp.where(base, m, 0.0) for m in ms]
    ts = [eye + p for p in ps]
    s = 1
    while 2 * s < INV_BASE:
        ps = [_bdot(p, p) for p in ps]
        ts = [t + _bdot(t, p) for t, p in zip(ts, ps)]
        s *= 2
    s = INV_BASE
    while s < n:
        off = same_block(2 * s) & jnp.logical_not(same_block(s))
        cts = [_bdot(jnp.where(off, m, 0.0), t) for m, t in zip(ms, ts)]
        ts = [t + _bdot(t, ct) for t, ct in zip(ts, cts)]
        s *= 2
    return ts


def _mixer_kernel(z_ref, conv_ref, gbias_ref, mnorm_ref, mu_ref, w0_ref, a0_ref, wa_ref, gup_ref,
                  kk_ref, ka_ref, rk_ref, lnw_ref, lnb_ref, seg_ref,
                  y_ref,
                  qkbuf, rwbuf, ct_ref, n_ref, m_ref, s_ref):
    L = CHUNK
    t_idx = pl.program_id(1)

    @pl.when(t_idx == 0)
    def _():
        qkbuf[...] = jnp.zeros_like(qkbuf)
        rwbuf[...] = jnp.zeros_like(rwbuf)
        ct_ref[...] = jnp.zeros_like(ct_ref)
        n_ref[...] = jnp.zeros_like(n_ref)
        m_ref[...] = jnp.zeros_like(m_ref)
        s_ref[...] = jnp.zeros_like(s_ref)

    @pl.when(t_idx > 0)
    def _():
        qkbuf[0:8, :] = qkbuf[L:L + 8, :]
        rwbuf[0:8, :] = rwbuf[L:L + 8, :]

    qkbuf[8:8 + L, :] = z_ref[0, :, C_QK:C_QK + 2 * D_MLSTM]
    rwbuf[8:8 + L, :] = z_ref[0, :, C_RW:C_END]

    row = lax.broadcasted_iota(jnp.int32, (L, L), 0)
    col = lax.broadcasted_iota(jnp.int32, (L, L), 1)
    lower = row >= col
    strict = row > col
    tri = lower.astype(BF16)

    conv_w = conv_ref[...]
    qk = qkbuf[8:8 + L, :] * conv_w[CONV_WIDTH - 1:CONV_WIDTH, :]
    for j in range(1, CONV_WIDTH):
        qk += qkbuf[8 - j:8 - j + L, :] * conv_w[CONV_WIDTH - 1 - j:CONV_WIDTH - j, :]
    qk = qk * _sigmoid(qk)
    q_all = qk[:, :D_MLSTM]
    k_all = qk[:, D_MLSTM:] * (MLSTM_HEAD_DIM ** -0.5)

    gates = z_ref[0, :, C_MG:C_MG + 128] + gbias_ref[...]
    bcum = _exact_left_dot(tri, _log_sigmoid(gates))
    gates_t = gates.T
    bcum_t = bcum.T

    for h in range(MLSTM_HEADS):
        hs = slice(h * MLSTM_HEAD_DIM, (h + 1) * MLSTM_HEAD_DIM)
        qh = q_all[:, hs]
        kh = k_all[:, hs]
        vh = z_ref[0, :, C_MV + h * MLSTM_HEAD_DIM:C_MV + (h + 1) * MLSTM_HEAD_DIM]
        oh = z_ref[0, :, C_MO + h * MLSTM_HEAD_DIM:C_MO + (h + 1) * MLSTM_HEAD_DIM]
        f_lane = MLSTM_HEADS + h
        b_col = bcum[:, f_lane:f_lane + 1]
        b_row = bcum_t[f_lane:f_lane + 1, :]
        ig_col = gates[:, h:h + 1]
        ig_row = gates_t[h:h + 1, :]
        g_tot = bcum[L - 1:L, f_lane:f_lane + 1]
        m_prev = m_ref[h][:, 0:1]
        ct_prev = ct_ref[h]
        n_prev = n_ref[h]

        dm = jnp.where(lower, b_col - b_row + ig_row, -jnp.inf)
        m_intra = jnp.max(dm, axis=-1, keepdims=True)
        m_inter = b_col + m_prev
        m_t = jnp.maximum(m_inter, m_intra)
        s_mat = _bdot_nt(qh, kh) * jnp.exp(dm - m_t)
        s_inter = jnp.exp(m_inter - m_t)
        num = _bdot(s_mat, vh) + s_inter * _bdot(qh, ct_prev)
        den = (jnp.sum(s_mat, axis=-1, keepdims=True)
               + s_inter * jnp.sum(qh * n_prev, axis=-1, keepdims=True))
        hh = num / jnp.maximum(jnp.abs(den), jnp.exp(-m_t))

        a_col = g_tot - b_col + ig_col
        m_loc = jnp.max(a_col, axis=0, keepdims=True)
        wgt = jnp.exp(a_col - m_loc)
        d_ct = _bdot_tn(kh, vh * wgt)
        d_n = jnp.sum(kh * wgt, axis=0, keepdims=True)
        m_new = jnp.maximum(g_tot + m_prev, m_loc)
        s_old = jnp.exp(g_tot + m_prev - m_new)
        s_new = jnp.exp(m_loc - m_new)
        ct_ref[h] = s_old * ct_prev + s_new * d_ct
        n_ref[h] = s_old * n_prev + s_new * d_n
        m_ref[h] = jnp.broadcast_to(m_new, (1, 128))

        hh = hh * lax.rsqrt(jnp.mean(hh * hh, axis=-1, keepdims=True) + EPS)
        hh = hh * mnorm_ref[:, hs] * _sigmoid(oh)
        y_ref[0, :, hs] = hh.astype(y_ref.dtype)

    z_now = rwbuf[8:8 + L, :]
    z_prev = rwbuf[7:7 + L, :]
    zm = z_now + (z_prev - z_now) * mu_ref[...]
    r = zm[:, 0:D_RWKV]
    k = zm[:, D_RWKV:2 * D_RWKV]
    v = zm[:, 2 * D_RWKV:3 * D_RWKV]
    xwa = zm[:, 3 * D_RWKV:3 * D_RWKV + 128]
    xg = zm[:, 3 * D_RWKV + 128:3 * D_RWKV + 256]
    lane128 = lax.broadcasted_iota(jnp.int32, (L, 128), 1)
    lora_in = jnp.where(lane128 < DECAY_LORA, jnp.tanh(xwa), xwa)
    lora = _bdot(lora_in, wa_ref[...])
    w_raw = -_softplus(-(w0_ref[...] + lora[:, :D_RWKV])) - 0.5
    logw = -jnp.exp(w_raw)
    iclr = _sigmoid(a0_ref[...] + lora[:, D_RWKV:])
    gate = _bdot(_sigmoid(xg), gup_ref[...])

    seg = seg_ref[...]
    kk = k * kk_ref[...]
    kk = kk / jnp.maximum(jnp.sqrt(_exact_right_dot(kk * kk, seg)), 1e-12)
    k2 = k * (1.0 + (iclr - 1.0) * ka_ref[...])
    a_vec = -kk
    b_vec = kk * iclr

    cum = _exact_left_dot(tri, logw)
    c_ref = cum[L // 2 - 1:L // 2, :]
    c_end = cum[L - 1:L, :]
    e_pos = jnp.exp(cum - c_ref)
    e_neg = jnp.exp(c_ref - cum)
    e_abs = jnp.exp(cum)
    e_tail = jnp.exp(c_end - cum)
    e_shift = jnp.exp(-logw)
    r_rel = r * e_pos
    a_rel = a_vec * e_pos * e_shift
    k_rel = k2 * e_neg
    b_rel = b_vec * e_neg
    r_abs = r * e_abs
    a_abs = a_vec * e_abs * e_shift
    k_tail = k2 * e_tail
    b_tail = b_vec * e_tail
    e_end = jnp.exp(c_end)

    lane_lo = lane128 < RWKV_HEAD_DIM
    lane_lo2 = lax.broadcasted_iota(jnp.int32, (2 * L, 128), 1) < RWKV_HEAD_DIM
    blockdiag = ((row // RWKV_HEAD_DIM) == (col // RWKV_HEAD_DIM)).astype(F32)

    pair_slices = [slice(p * 128, (p + 1) * 128) for p in range(RWKV_PAIRS)]
    m_ab, m_ak, m_r = [], [], []
    for ps in pair_slices:
        ar_rel = jnp.concatenate([a_rel[:, ps], r_rel[:, ps]], axis=0)
        bk_rel = jnp.concatenate([b_rel[:, ps], k_rel[:, ps]], axis=0)
        for sel in (lane_lo2, jnp.logical_not(lane_lo2)):
            g_mat = _bdot_nt(jnp.where(sel, ar_rel, 0.0), bk_rel)
            m_ab.append(jnp.where(strict, g_mat[:L, :L], 0.0))
            m_ak.append(jnp.where(strict, g_mat[:L, L:], 0.0).astype(BF16))
            m_r.append(jnp.concatenate([jnp.where(lower, g_mat[L:, :L], 0.0),
                                        jnp.where(lower, g_mat[L:, L:], 0.0)], axis=1).astype(BF16))
    t_inv = _unit_lower_inverses(m_ab, row, col)

    pairs = range(RWKV_PAIRS)
    v_ps = [v[:, ps].astype(BF16) for ps in pair_slices]
    s_prev = [s_ref[p] for p in pairs]
    inter = [_bdot_nt(jnp.concatenate([a_abs[:, ps], r_abs[:, ps]], axis=0), s_prev[p])
             for p, ps in enumerate(pair_slices)]
    rhs = [inter[p][:L] + jnp.where(lane_lo, _bdot(m_ak[2 * p], v_ps[p]), _bdot(m_ak[2 * p + 1], v_ps[p]))
           for p in pairs]
    u_ps = [jnp.where(lane_lo, _bdot(t_inv[2 * p], rhs[p]), _bdot(t_inv[2 * p + 1], rhs[p])) for p in pairs]
    uv = [jnp.concatenate([u_ps[p].astype(BF16), v_ps[p]], axis=0) for p in pairs]
    y_parts = [inter[p][L:] + jnp.where(lane_lo, _bdot(m_r[2 * p], uv[p]), _bdot(m_r[2 * p + 1], uv[p]))
               for p in pairs]
    for p, ps in enumerate(pair_slices):
        bk_tail = jnp.concatenate([b_tail[:, ps], k_tail[:, ps]], axis=0)
        s_ref[p] = s_prev[p] * e_end[:, ps] + blockdiag * _bdot_tn(uv[p], bk_tail)

    y = jnp.concatenate(y_parts, axis=1)
    inv_n = 1.0 / RWKV_HEAD_DIM
    mean = _exact_right_dot(y, seg) * inv_n
    yc = y - mean
    var = _exact_right_dot(yc * yc, seg) * inv_n
    y = yc * lax.rsqrt(var + RWKV_LN_EPS) * lnw_ref[...] + lnb_ref[...]
    y = y + _exact_right_dot(r * k2 * rk_ref[...], seg) * v
    y_ref[0, :, D_MLSTM:] = (y * gate).astype(y_ref.dtype)


def _mixer(z, conv_w, gbias, mnorm, mu, w0, a0, wa, gup, kk, ka, rk, lnw, lnb, seg):
    b, t, c = z.shape
    nt = t // CHUNK
    const = lambda a: pl.BlockSpec(a.shape, lambda i, j: (0,) * a.ndim)
    params = (conv_w, gbias, mnorm, mu, w0, a0, wa, gup, kk, ka, rk, lnw, lnb, seg)
    return pl.pallas_call(
        _mixer_kernel,
        grid=(b, nt),
        in_specs=[pl.BlockSpec((1, CHUNK, c), lambda i, j: (i, j, 0))] + [const(a) for a in params],
        out_specs=pl.BlockSpec((1, CHUNK, D_MODEL), lambda i, j: (i, j, 0)),
        out_shape=jax.ShapeDtypeStruct((b, t, D_MODEL), BF16),
        scratch_shapes=[
            pltpu.VMEM((CHUNK + 8, 2 * D_MLSTM), F32),
            pltpu.VMEM((CHUNK + 8, D_RW_IN), F32),
            pltpu.VMEM((MLSTM_HEADS, MLSTM_HEAD_DIM, MLSTM_HEAD_DIM), F32),
            pltpu.VMEM((MLSTM_HEADS, 1, MLSTM_HEAD_DIM), F32),
            pltpu.VMEM((MLSTM_HEADS, 1, 128), F32),
            pltpu.VMEM((RWKV_PAIRS, 128, 128), F32),
        ],
        compiler_params=pltpu.CompilerParams(dimension_semantics=("parallel", "arbitrary"),
                                             vmem_limit_bytes=VMEM_LIMIT),
        name="mixer",
    )(z, *params)


def _tail_kernel(x_ref, y_ref, kv_ref, wmix_ref, gx_ref, wq_ref, wo_ref, gf_ref, wg_ref, wu_ref, wd_ref,
                 gfin_ref, o_ref):
    h = x_ref[...] + jnp.dot(y_ref[...], wmix_ref[...], preferred_element_type=F32)

    q = _bdot(_rms(h, gx_ref[...]), wq_ref[...])
    heads = []
    for i in range(XATTN_HEADS):
        hs = slice(i * XATTN_HEAD_DIM, (i + 1) * XATTN_HEAD_DIM)
        k_h = kv_ref[0, :, i * XATTN_HEAD_DIM:(i + 1) * XATTN_HEAD_DIM]
        v_h = kv_ref[0, :, D_MODEL + i * XATTN_HEAD_DIM:D_MODEL + (i + 1) * XATTN_HEAD_DIM]
        s = _bdot_nt(q[:, hs], k_h) * (XATTN_HEAD_DIM ** -0.5)
        e = jnp.exp(s - jnp.max(s, axis=-1, keepdims=True))
        heads.append(_bdot(e, v_h) / jnp.sum(e, axis=-1, keepdims=True))
    h = h + _bdot(jnp.concatenate(heads, axis=1), wo_ref[...])

    u = _rms(h, gf_ref[...]).astype(BF16)
    gate = jnp.dot(u, wg_ref[...], preferred_element_type=F32)
    up = jnp.dot(u, wu_ref[...], preferred_element_type=F32)
    h = h + _bdot(gate * _sigmoid(gate) * up, wd_ref[...])

    o_ref[...] = _rms(h, gfin_ref[...])


def _tail(x2, y2, kv, wmix, gx, wq, wo, gf, wg, wu, wd, gfin, tm, t):
    n, d = x2.shape
    per_b = t // tm
    const = lambda a: pl.BlockSpec(a.shape, lambda i: (0,) * a.ndim, pipeline_mode=pl.Buffered(1))
    params = (wmix, gx, wq, wo, gf, wg, wu, wd, gfin)
    return pl.pallas_call(
        _tail_kernel,
        grid=(n // tm,),
        in_specs=[pl.BlockSpec((tm, d), lambda i: (i, 0)),
                  pl.BlockSpec((tm, d), lambda i: (i, 0)),
                  pl.BlockSpec((1,) + kv.shape[1:], lambda i: (i // per_b, 0, 0))]
                 + [const(a) for a in params],
        out_specs=pl.BlockSpec((tm, d), lambda i: (i, 0)),
        out_shape=jax.ShapeDtypeStruct((n, d), F32),
        compiler_params=pltpu.CompilerParams(dimension_semantics=("parallel",),
                                             vmem_limit_bytes=VMEM_LIMIT),
        name="tail",
    )(x2, y2, kv, *params)


def _regroup_w_in(w):
    d = w.shape[0]
    n_m = 4 * D_MLSTM
    gates = w[:, n_m:n_m + 2 * MLSTM_HEADS]
    gates = jnp.pad(gates, ((0, 0), (0, 128 - 2 * MLSTM_HEADS)))
    return jnp.concatenate([w[:, :n_m], gates, w[:, n_m + 2 * MLSTM_HEADS:]], axis=1).astype(BF16)


def kernel(x, mem, norm_mix, w_in, mlstm_conv, mlstm_i_bias, mlstm_f_bias, mlstm_norm, rwkv_mu, rwkv_w0, rwkv_w_up, rwkv_a0, rwkv_a_up, rwkv_g_up, rwkv_k_k, rwkv_k_a, rwkv_r_k, rwkv_ln_w, rwkv_ln_b, w_mix_out, norm_xattn, norm_mem, xattn_wq, xattn_wkv, xattn_wo, norm_ffn, ffn_w_gate, ffn_w_up, ffn_w_down, norm_final):
    b, t, d = x.shape
    assert d == D_MODEL and t % CHUNK == 0 and norm_mix.shape[0] == 1
    row = lambda a: a.reshape(1, -1).astype(F32)
    tm = 256 if t % 256 == 0 else CHUNK
    x2 = x.reshape(b * t, d)

    z = _in_proj(x2, row(norm_mix[0]), _regroup_w_in(w_in[0]), tm).reshape(b, t, C_END)
    kv = _mem_kv(mem, row(norm_mem[0]), xattn_wkv[0].astype(BF16))

    gbias = jnp.pad(jnp.concatenate([mlstm_i_bias[0], mlstm_f_bias[0]]), (0, 128 - 2 * MLSTM_HEADS))
    zeros = jnp.zeros((DECAY_LORA, D_RWKV), F32)
    wa = jnp.concatenate([jnp.concatenate([rwkv_w_up[0], zeros], axis=1),
                          jnp.concatenate([zeros, rwkv_a_up[0]], axis=1)], axis=0).astype(BF16)
    head_id = jnp.arange(D_RWKV) // RWKV_HEAD_DIM
    seg = (head_id[:, None] == head_id[None, :]).astype(BF16)
    y = _mixer(z, mlstm_conv[0], row(gbias), row(mlstm_norm[0]), row(rwkv_mu[0]), row(rwkv_w0[0]),
               row(rwkv_a0[0]), wa, rwkv_g_up[0].astype(BF16), row(rwkv_k_k[0]), row(rwkv_k_a[0]),
               row(rwkv_r_k[0]), row(rwkv_ln_w[0]), row(rwkv_ln_b[0]), seg)

    out = _tail(x2, y.reshape(b * t, d), kv, w_mix_out[0].astype(BF16), row(norm_xattn[0]),
                xattn_wq[0].astype(BF16), xattn_wo[0].astype(BF16), row(norm_ffn[0]),
                ffn_w_gate[0].astype(BF16), ffn_w_up[0].astype(BF16), ffn_w_down[0].astype(BF16),
                row(norm_final), tm, t)
    return out.reshape(b, t, d)
```

```python
import functools

import jax
import jax.numpy as jnp
from jax import lax
from jax.experimental import pallas as pl
from jax.experimental.pallas import tpu as pltpu

F32 = jnp.float32
BF16 = jnp.bfloat16

EPS = 1e-6
RWKV_LN_EPS = 64e-5
CONV_WIDTH = 4

D_MODEL = 1024
D_MLSTM = 512
D_RWKV = 512
MLSTM_HEADS = 4
MLSTM_HEAD_DIM = 128
RWKV_HEAD_DIM = 64
RWKV_PAIRS = D_RWKV // (2 * RWKV_HEAD_DIM)
DECAY_LORA = 64
ICLR_LORA = 64
GATE_LORA = 128
XATTN_HEADS = 4
XATTN_HEAD_DIM = 256
CHUNK = 128
INV_BASE = 16
PROJ_COLS = 512
STREAMS = 2
STREAM_STAGGER = 5

MASK_LOWER, MASK_STRICT = 0, 1
MASK_SAME = {16: 2, 32: 3, 64: 4}
MASK_OFF = {16: 5, 32: 6, 64: 7}

C_QK = 0
C_MV = C_QK + 2 * D_MLSTM
C_MO = C_MV + D_MLSTM
C_MG = C_MO + D_MLSTM
C_RW = C_MG + 128
D_RW_IN = 3 * D_RWKV + DECAY_LORA + ICLR_LORA + GATE_LORA
C_END = C_RW + D_RW_IN

VMEM_LIMIT = 56 * 1024 * 1024


def _bdot(a, b):
    return jnp.dot(a.astype(BF16), b.astype(BF16), preferred_element_type=F32)


def _bdot_nt(a, b):
    return lax.dot_general(a.astype(BF16), b.astype(BF16), (((1,), (1,)), ((), ())),
                           preferred_element_type=F32)


def _bdot_tn(a, b):
    return lax.dot_general(a.astype(BF16), b.astype(BF16), (((0,), (0,)), ((), ())),
                           preferred_element_type=F32)


def _split3(x):
    hi = x.astype(BF16)
    r1 = x - hi.astype(F32)
    mid = r1.astype(BF16)
    lo = (r1 - mid.astype(F32)).astype(BF16)
    return hi, mid, lo


def _exact_left_dot(ones_mat, x):
    hi, mid, lo = _split3(x)
    acc = jnp.dot(ones_mat, hi, preferred_element_type=F32)
    acc += jnp.dot(ones_mat, mid, preferred_element_type=F32)
    acc += jnp.dot(ones_mat, lo, preferred_element_type=F32)
    return acc


def _head_sums(x, seg):
    n, w = x.shape
    hi = x.astype(BF16)
    lo = (x - hi.astype(F32)).astype(BF16)
    stacked = jnp.concatenate([hi[:, :w // 2], hi[:, w // 2:], lo[:, :w // 2], lo[:, w // 2:]], axis=0)
    sums = jnp.dot(stacked, seg, preferred_element_type=F32)
    return jnp.concatenate([sums[0:n] + sums[2 * n:3 * n], sums[n:2 * n] + sums[3 * n:4 * n]], axis=1)


def _rms(x, g):
    return x * lax.rsqrt(jnp.mean(x * x, axis=-1, keepdims=True) + EPS) * g


def _sigmoid(x):
    return 1.0 / (1.0 + jnp.exp(-x))


def _log_sigmoid(x):
    return jnp.minimum(x, 0.0) - jnp.log(1.0 + jnp.exp(-jnp.abs(x)))


def _softplus(x):
    return jnp.maximum(x, 0.0) + jnp.log(1.0 + jnp.exp(-jnp.abs(x)))


def _mem_kv_kernel(m_ref, g_ref, w_ref, kv_ref):
    u = _rms(m_ref[0], g_ref[...]).astype(BF16)
    kv_ref[0] = jnp.dot(u, w_ref[...], preferred_element_type=F32).astype(BF16)


def _mem_kv(mem, g, w):
    b, m, d = mem.shape
    c = w.shape[1]
    return pl.pallas_call(
        _mem_kv_kernel,
        grid=(b,),
        in_specs=[pl.BlockSpec((1, m, d), lambda i: (i, 0, 0)),
                  pl.BlockSpec((1, d), lambda i: (0, 0)),
                  pl.BlockSpec((d, c), lambda i: (0, 0), pipeline_mode=pl.Buffered(1))],
        out_specs=pl.BlockSpec((1, m, c), lambda i: (i, 0, 0)),
        out_shape=jax.ShapeDtypeStruct((b, m, c), BF16),
        compiler_params=pltpu.CompilerParams(dimension_semantics=("parallel",),
                                             vmem_limit_bytes=VMEM_LIMIT),
        name="mem_kv",
    )(mem, g, w)


def _interleave(*gens):
    results = [None] * len(gens)
    live = list(enumerate(gens))
    while live:
        still = []
        for i, g in live:
            try:
                next(g)
                still.append((i, g))
            except StopIteration as stop:
                results[i] = stop.value
        live = still
    return results


def _interleaved(*gens):
    results = [None] * len(gens)
    live = list(enumerate(gens))
    while live:
        still = []
        for i, g in live:
            try:
                next(g)
                still.append((i, g))
            except StopIteration as stop:
                results[i] = stop.value
        live = still
        yield
    return results


def _delayed(gen, steps):
    for _ in range(steps):
        yield
    return (yield from gen)


def _fold_rows(x, s):
    acc = x[0:s]
    for j in range(1, x.shape[0] // s):
        acc = acc + x[j * s:(j + 1) * s]
    return acc


def _tile_rows(xc, n):
    return jnp.concatenate([xc] * (n // xc.shape[0]), axis=0)


def _unit_lower_inverses(ms, same_blk, off_blk):
    n = ms[0].shape[0]
    b = INV_BASE
    dot = functools.partial(jnp.dot, preferred_element_type=F32)
    lane = lax.broadcasted_iota(jnp.int32, (b, 128), 1)
    eye_c = (lane % b == lax.broadcasted_iota(jnp.int32, (b, 128), 0)).astype(F32)

    m_diag = [m * same_blk[b] for m in ms]
    pcs = [_fold_rows(m, b) for m in m_diag]
    tcs = [eye_c + p.astype(F32) for p in pcs]
    pcs = [dot(p, m) for p, m in zip(pcs, m_diag)]
    yield
    s = 2
    while 2 * s < b:
        pfs = [_tile_rows(p.astype(BF16), n) * same_blk[b] for p in pcs]
        xs = [dot(jnp.concatenate([t, p], axis=0).astype(BF16), f) for t, p, f in zip(tcs, pcs, pfs)]
        tcs = [t + x[:b] for t, x in zip(tcs, xs)]
        pcs = [x[b:] for x in xs]
        s *= 2
        yield
    pfs = [_tile_rows(p.astype(BF16), n) * same_blk[b] for p in pcs]
    tcs = [t + dot(t.astype(BF16), f) for t, f in zip(tcs, pfs)]
    yield
    s = b
    while s < n:
        tfs = [_tile_rows(t.astype(BF16), n) * same_blk[s] for t in tcs]
        ccs = [_fold_rows(m * off_blk[s], s) for m in ms]
        x1s = [dot(c, f) for c, f in zip(ccs, tfs)]
        yield
        ws = [_tile_rows(x.astype(BF16), n) * off_blk[s] for x in x1s]
        x2s = [dot(t.astype(BF16), w) for t, w in zip(tcs, ws)]
        even = (lax.broadcasted_iota(jnp.int32, (s, 128), 1) // s) % 2 == 0
        tcs = [jnp.concatenate([jnp.where(even, t, 0.0), x + jnp.where(even, 0.0, t)], axis=0)
               for t, x in zip(tcs, x2s)]
        yield
        s *= 2
    return tcs


def _mlstm_chunk(zvo, qkbuf, conv_ref, gbias_ref, mnorm_ref, states, tri, lower):
    L = CHUNK
    conv_w = conv_ref[...]
    qk = qkbuf[8:8 + L, :] * conv_w[CONV_WIDTH - 1:CONV_WIDTH, :]
    for j in range(1, CONV_WIDTH):
        qk += qkbuf[8 - j:8 - j + L, :] * conv_w[CONV_WIDTH - 1 - j:CONV_WIDTH - j, :]
    yield
    qk = qk * _sigmoid(qk)
    q_all = qk[:, :D_MLSTM]
    k_all = qk[:, D_MLSTM:] * (MLSTM_HEAD_DIM ** -0.5)
    yield

    gates = zvo[:, C_MG - C_MV:C_MG - C_MV + 128] + gbias_ref[...]
    bcum = _exact_left_dot(tri, _log_sigmoid(gates))
    gates_t = gates.T
    bcum_t = bcum.T

    outs, new_states = [], []
    for h in range(MLSTM_HEADS):
        hs = slice(h * MLSTM_HEAD_DIM, (h + 1) * MLSTM_HEAD_DIM)
        qh = q_all[:, hs].astype(BF16)
        kh = k_all[:, hs]
        vh = zvo[:, h * MLSTM_HEAD_DIM:(h + 1) * MLSTM_HEAD_DIM]
        oh = zvo[:, C_MO - C_MV + h * MLSTM_HEAD_DIM:C_MO - C_MV + (h + 1) * MLSTM_HEAD_DIM]
        f_lane = MLSTM_HEADS + h
        b_col = bcum[:, f_lane:f_lane + 1]
        b_row = bcum_t[f_lane:f_lane + 1, :]
        ig_col = gates[:, h:h + 1]
        ig_row = gates_t[h:h + 1, :]
        g_tot = bcum[L - 1:L, f_lane:f_lane + 1]
        ct_prev, n_prev, m_prev = states[h]
        m_prev = m_prev[:, 0:1]

        dm = jnp.where(lower, b_col - b_row + ig_row, -jnp.inf)
        m_intra = jnp.max(dm, axis=-1, keepdims=True)
        m_inter = b_col + m_prev
        m_t = jnp.maximum(m_inter, m_intra)
        s_mat = _bdot_nt(qh, kh) * jnp.exp(dm - m_t)
        s_inter = jnp.exp(m_inter - m_t)
        num = _bdot(s_mat, vh) + s_inter * _bdot(qh, ct_prev)
        den = (jnp.sum(s_mat, axis=-1, keepdims=True)
               + s_inter * jnp.sum(q_all[:, hs] * n_prev, axis=-1, keepdims=True))
        hh = num / jnp.maximum(jnp.abs(den), jnp.exp(-m_t))
        yield

        a_col = g_tot - b_col + ig_col
        m_loc = jnp.max(a_col, axis=0, keepdims=True)
        wgt = jnp.exp(a_col - m_loc)
        d_ct = _bdot_tn(kh, vh * wgt)
        d_n = jnp.sum(kh * wgt, axis=0, keepdims=True)
        m_new = jnp.maximum(g_tot + m_prev, m_loc)
        s_old = jnp.exp(g_tot + m_prev - m_new)
        s_new = jnp.exp(m_loc - m_new)
        new_states.append((s_old * ct_prev + s_new * d_ct, s_old * n_prev + s_new * d_n,
                           jnp.broadcast_to(m_new, (1, 128))))

        hh = hh * lax.rsqrt(jnp.mean(hh * hh, axis=-1, keepdims=True) + EPS)
        outs.append(hh * mnorm_ref[:, hs] * _sigmoid(oh))
        yield
    return outs, new_states


def _project_next(x_ref, g_ref, w_ref, znext):
    u = _rms(x_ref[...].reshape(STREAMS * CHUNK, D_MODEL), g_ref[...]).astype(BF16)
    yield
    for c0 in range(0, C_END, PROJ_COLS):
        c1 = min(c0 + PROJ_COLS, C_END)
        znext[:, c0:c1] = jnp.dot(u, w_ref[:, c0:c1], preferred_element_type=F32)
        yield


def _chunk_stream(zvo, qkbuf, rwbuf, s_prev, mlstm_states, prm, masks_ref):
    L = CHUNK
    pairs = range(RWKV_PAIRS)
    tri = masks_ref[MASK_LOWER]
    lower = tri > 0
    strict = masks_ref[MASK_STRICT] > 0
    same_blk = {s: masks_ref[MASK_SAME[s]] for s in MASK_SAME}
    off_blk = {s: masks_ref[MASK_OFF[s]] for s in MASK_OFF}

    z_now = rwbuf[8:8 + L, :]
    z_prev = rwbuf[7:7 + L, :]
    zm = z_now + (z_prev - z_now) * prm["mu"][...]
    r = zm[:, 0:D_RWKV]
    k = zm[:, D_RWKV:2 * D_RWKV]
    v = zm[:, 2 * D_RWKV:3 * D_RWKV]
    xwa = zm[:, 3 * D_RWKV:3 * D_RWKV + 128]
    xg = zm[:, 3 * D_RWKV + 128:3 * D_RWKV + 256]
    lane128 = lax.broadcasted_iota(jnp.int32, (L, 128), 1)
    lora_in = jnp.where(lane128 < DECAY_LORA, jnp.tanh(xwa), xwa)
    yield
    lora = _bdot(lora_in, prm["wa"][...])
    w_raw = -_softplus(-(prm["w0"][...] + lora[:, :D_RWKV])) - 0.5
    logw = -jnp.exp(w_raw)
    iclr = _sigmoid(prm["a0"][...] + lora[:, D_RWKV:])
    gate = _bdot(_sigmoid(xg), prm["gup"][...])
    yield

    seg = prm["seg"][...]
    kk = k * prm["kk"][...]
    k2 = k * (1.0 + (iclr - 1.0) * prm["ka"][...])
    seg_sums = _head_sums(jnp.concatenate([kk * kk, r * k2 * prm["rk"][...]], axis=0), seg)
    kk = kk / jnp.maximum(jnp.sqrt(seg_sums[:L]), 1e-12)
    bonus = seg_sums[L:]
    a_vec = -kk
    b_vec = kk * iclr
    yield

    cum = _exact_left_dot(tri, logw)
    c_ref = cum[L // 2 - 1:L // 2, :]
    c_end = cum[L - 1:L, :]
    e_pos = jnp.exp(cum - c_ref)
    e_neg = jnp.exp(c_ref - cum)
    r_rel = r * e_pos
    a_rel = a_vec * jnp.exp(-logw) * e_pos
    k_rel = k2 * e_neg
    b_rel = b_vec * e_neg
    e_ref = jnp.exp(c_ref)
    e_tail = jnp.exp(c_end - c_ref)
    e_end = jnp.exp(c_end)
    yield

    lane_lo = lane128 < RWKV_HEAD_DIM
    lane_lo2 = lax.broadcasted_iota(jnp.int32, (2 * L, 128), 1) < RWKV_HEAD_DIM
    blockdiag = same_blk[RWKV_HEAD_DIM].astype(F32)

    pair_slices = [slice(p * 128, (p + 1) * 128) for p in pairs]
    m_ab, m_ak, m_r = [], [], []
    ar_rel, bk_rel = [], []
    zero = jnp.zeros((), BF16)
    for ps in pair_slices:
        ar_rel.append(jnp.concatenate([a_rel[:, ps], r_rel[:, ps]], axis=0).astype(BF16))
        bk_rel.append(jnp.concatenate([b_rel[:, ps], k_rel[:, ps]], axis=0).astype(BF16))
        ar_split = jnp.concatenate([jnp.where(lane_lo2, ar_rel[-1], zero),
                                    jnp.where(lane_lo2, zero, ar_rel[-1])], axis=0)
        g_both = _bdot_nt(ar_split, bk_rel[-1]).astype(BF16)
        for half in range(2):
            g_mat = g_both[2 * L * half:2 * L * (half + 1)]
            m_ab.append(jnp.where(strict, g_mat[:L, :L], zero))
            m_ak.append(jnp.where(strict, g_mat[:L, L:], zero))
            m_r.append(jnp.concatenate([jnp.where(lower, g_mat[L:, :L], zero),
                                        jnp.where(lower, g_mat[L:, L:], zero)], axis=1))
    yield
    t_inv, (m_outs, m_new) = yield from _interleaved(
        _unit_lower_inverses(m_ab, same_blk, off_blk),
        _mlstm_chunk(zvo, qkbuf, prm["conv"], prm["gbias"], prm["mnorm"], mlstm_states, tri, lower))

    def both_heads(stacked):
        return jnp.where(lane_lo, stacked[:L], stacked[L:])

    v_ps = [v[:, ps].astype(BF16) for ps in pair_slices]
    inter = [_bdot_nt(ar_rel[p], s_prev[p] * e_ref[:, ps])
             for p, ps in enumerate(pair_slices)]
    rhs = [inter[p][:L] + both_heads(_bdot(jnp.concatenate([m_ak[2 * p], m_ak[2 * p + 1]], axis=0), v_ps[p]))
           for p in pairs]
    yield
    u_ps = [both_heads(_bdot(jnp.concatenate([t_inv[2 * p].astype(BF16), t_inv[2 * p + 1].astype(BF16)], axis=0),
                             rhs[p])) for p in pairs]
    uv = [jnp.concatenate([u_ps[p].astype(BF16), v_ps[p]], axis=0) for p in pairs]
    yield
    y_parts = [inter[p][L:] + both_heads(_bdot(jnp.concatenate([m_r[2 * p], m_r[2 * p + 1]], axis=0), uv[p]))
               for p in pairs]
    s_new = [s_prev[p] * e_end[:, ps] + (blockdiag * e_tail[:, ps]) * _bdot_tn(uv[p], bk_rel[p])
             for p, ps in enumerate(pair_slices)]
    yield

    y = jnp.concatenate(y_parts, axis=1)
    inv_n = 1.0 / RWKV_HEAD_DIM
    mean = _head_sums(y, seg) * inv_n
    yc = y - mean
    var = _head_sums(yc * yc, seg) * inv_n
    y = yc * lax.rsqrt(var + RWKV_LN_EPS) * prm["lnw"][...] + prm["lnb"][...]
    y = (y + bonus * v) * gate
    return m_outs, m_new, y, s_new


MIXER_PARAMS = ("conv", "gbias", "mnorm", "mu", "w0", "a0", "wa", "gup", "kk", "ka", "rk", "lnw", "lnb", "seg")


def _mixer_kernel(xn_ref, x0_ref, gin_ref, win_ref, *rest):
    prm = dict(zip(MIXER_PARAMS, rest[:len(MIXER_PARAMS)]))
    masks_ref, y_ref, znext, zvo, qkbuf, rwbuf, ct_ref, n_ref, m_ref, s_ref = rest[len(MIXER_PARAMS):]
    L = CHUNK
    t_idx = pl.program_id(1)

    @pl.when(t_idx == 0)
    def _():
        u0 = _rms(x0_ref[...].reshape(STREAMS * L, D_MODEL), gin_ref[...]).astype(BF16)
        znext[...] = jnp.dot(u0, win_ref[...], preferred_element_type=F32)
        qkbuf[...] = jnp.zeros_like(qkbuf)
        rwbuf[...] = jnp.zeros_like(rwbuf)
        ct_ref[...] = jnp.zeros_like(ct_ref)
        n_ref[...] = jnp.zeros_like(n_ref)
        m_ref[...] = jnp.zeros_like(m_ref)
        s_ref[...] = jnp.zeros_like(s_ref)

    @pl.when(t_idx > 0)
    def _():
        for q in range(STREAMS):
            qkbuf[q, 0:8, :] = qkbuf[q, L:L + 8, :]
            rwbuf[q, 0:8, :] = rwbuf[q, L:L + 8, :]

    for q in range(STREAMS):
        rows = slice(q * L, (q + 1) * L)
        qkbuf[q, 8:8 + L, :] = znext[rows, C_QK:C_QK + 2 * D_MLSTM]
        zvo[q] = znext[rows, C_MV:C_RW]
        rwbuf[q, 8:8 + L, :] = znext[rows, C_RW:C_END]

    streams = []
    for q in range(STREAMS):
        s_prev = [s_ref[q, p] for p in range(RWKV_PAIRS)]
        mlstm_states = [(ct_ref[q, h], n_ref[q, h], m_ref[q, h]) for h in range(MLSTM_HEADS)]
        stream = _chunk_stream(zvo.at[q], qkbuf.at[q], rwbuf.at[q], s_prev, mlstm_states, prm, masks_ref)
        streams.append(_delayed(stream, q * STREAM_STAGGER))

    *results, _ = _interleave(*streams, _project_next(xn_ref, gin_ref, win_ref, znext))

    for q, (m_outs, m_new, y, s_new) in enumerate(results):
        for h in range(MLSTM_HEADS):
            y_ref[q, :, h * MLSTM_HEAD_DIM:(h + 1) * MLSTM_HEAD_DIM] = m_outs[h].astype(y_ref.dtype)
            ct_ref[q, h], n_ref[q, h], m_ref[q, h] = m_new[h]
        y_ref[q, :, D_MLSTM:] = y.astype(y_ref.dtype)
        for p in range(RWKV_PAIRS):
            s_ref[q, p] = s_new[p]


def _mixer(x, gin, w_in, conv_w, gbias, mnorm, mu, w0, a0, wa, gup, kk, ka, rk, lnw, lnb, seg, masks):
    b, t, d = x.shape
    nt = t // CHUNK
    const = lambda a: pl.BlockSpec(a.shape, lambda i, j: (0,) * a.ndim, pipeline_mode=pl.Buffered(1))
    params = (gin, w_in, conv_w, gbias, mnorm, mu, w0, a0, wa, gup, kk, ka, rk, lnw, lnb, seg, masks)
    return pl.pallas_call(
        _mixer_kernel,
        grid=(b // STREAMS, nt),
        in_specs=[pl.BlockSpec((STREAMS, CHUNK, d), lambda i, j: (i, jnp.minimum(j + 1, nt - 1), 0)),
                  pl.BlockSpec((STREAMS, CHUNK, d), lambda i, j: (i, 0, 0))] + [const(a) for a in params],
        out_specs=pl.BlockSpec((STREAMS, CHUNK, D_MODEL), lambda i, j: (i, j, 0)),
        out_shape=jax.ShapeDtypeStruct((b, t, D_MODEL), BF16),
        scratch_shapes=[
            pltpu.VMEM((STREAMS * CHUNK, C_END), F32),
            pltpu.VMEM((STREAMS, CHUNK, C_RW - C_MV), F32),
            pltpu.VMEM((STREAMS, CHUNK + 8, 2 * D_MLSTM), F32),
            pltpu.VMEM((STREAMS, CHUNK + 8, D_RW_IN), F32),
            pltpu.VMEM((STREAMS, MLSTM_HEADS, MLSTM_HEAD_DIM, MLSTM_HEAD_DIM), F32),
            pltpu.VMEM((STREAMS, MLSTM_HEADS, 1, MLSTM_HEAD_DIM), F32),
            pltpu.VMEM((STREAMS, MLSTM_HEADS, 1, 128), F32),
            pltpu.VMEM((STREAMS, RWKV_PAIRS, 128, 128), F32),
        ],
        compiler_params=pltpu.CompilerParams(dimension_semantics=("parallel", "arbitrary"),
                                             vmem_limit_bytes=VMEM_LIMIT),
        name="mixer",
    )(x, x, *params)


def _tail_kernel(x_ref, y_ref, kv_ref, wmix_ref, gx_ref, wq_ref, wo_ref, gf_ref, wg_ref, wu_ref, wd_ref,
                 gfin_ref, o_ref):
    h = x_ref[...] + jnp.dot(y_ref[...], wmix_ref[...], preferred_element_type=F32)

    q = _bdot(_rms(h, gx_ref[...]), wq_ref[...])
    heads = []
    for i in range(XATTN_HEADS):
        hs = slice(i * XATTN_HEAD_DIM, (i + 1) * XATTN_HEAD_DIM)
        k_h = kv_ref[0, :, i * XATTN_HEAD_DIM:(i + 1) * XATTN_HEAD_DIM]
        v_h = kv_ref[0, :, D_MODEL + i * XATTN_HEAD_DIM:D_MODEL + (i + 1) * XATTN_HEAD_DIM]
        s = _bdot_nt(q[:, hs], k_h) * (XATTN_HEAD_DIM ** -0.5)
        e = jnp.exp(s - jnp.max(s, axis=-1, keepdims=True))
        heads.append(_bdot(e, v_h) / jnp.sum(e, axis=-1, keepdims=True))
    h = h + _bdot(jnp.concatenate(heads, axis=1), wo_ref[...])

    u = _rms(h, gf_ref[...]).astype(BF16)
    gate = jnp.dot(u, wg_ref[...], preferred_element_type=F32)
    up = jnp.dot(u, wu_ref[...], preferred_element_type=F32)
    h = h + _bdot(gate * _sigmoid(gate) * up, wd_ref[...])

    o_ref[...] = _rms(h, gfin_ref[...])


def _tail(x2, y2, kv, wmix, gx, wq, wo, gf, wg, wu, wd, gfin, tm, t):
    n, d = x2.shape
    per_b = t // tm
    const = lambda a: pl.BlockSpec(a.shape, lambda i: (0,) * a.ndim, pipeline_mode=pl.Buffered(1))
    params = (wmix, gx, wq, wo, gf, wg, wu, wd, gfin)
    return pl.pallas_call(
        _tail_kernel,
        grid=(n // tm,),
        in_specs=[pl.BlockSpec((tm, d), lambda i: (i, 0)),
                  pl.BlockSpec((tm, d), lambda i: (i, 0)),
                  pl.BlockSpec((1,) + kv.shape[1:], lambda i: (i // per_b, 0, 0))]
                 + [const(a) for a in params],
        out_specs=pl.BlockSpec((tm, d), lambda i: (i, 0)),
        out_shape=jax.ShapeDtypeStruct((n, d), F32),
        compiler_params=pltpu.CompilerParams(dimension_semantics=("parallel",),
                                             vmem_limit_bytes=VMEM_LIMIT),
        name="tail",
    )(x2, y2, kv, *params)


def _chunk_masks():
    i = jnp.arange(CHUNK)
    row, col = i[:, None], i[None, :]
    planes = [None] * 8
    planes[MASK_LOWER] = row >= col
    planes[MASK_STRICT] = row > col
    for s, idx in MASK_SAME.items():
        planes[idx] = (row // s) == (col // s)
    for s, idx in MASK_OFF.items():
        planes[idx] = ((row // (2 * s)) == (col // (2 * s))) & ((row // s) > (col // s))
    return jnp.stack(planes).astype(BF16)


def _regroup_w_in(w):
    n_m = 4 * D_MLSTM
    gates = w[:, n_m:n_m + 2 * MLSTM_HEADS]
    gates = jnp.pad(gates, ((0, 0), (0, 128 - 2 * MLSTM_HEADS)))
    return jnp.concatenate([w[:, :n_m], gates, w[:, n_m + 2 * MLSTM_HEADS:]], axis=1).astype(BF16)


def kernel(x, mem, norm_mix, w_in, mlstm_conv, mlstm_i_bias, mlstm_f_bias, mlstm_norm, rwkv_mu, rwkv_w0, rwkv_w_up, rwkv_a0, rwkv_a_up, rwkv_g_up, rwkv_k_k, rwkv_k_a, rwkv_r_k, rwkv_ln_w, rwkv_ln_b, w_mix_out, norm_xattn, norm_mem, xattn_wq, xattn_wkv, xattn_wo, norm_ffn, ffn_w_gate, ffn_w_up, ffn_w_down, norm_final):
    b, t, d = x.shape
    assert d == D_MODEL and t % CHUNK == 0 and b % STREAMS == 0 and norm_mix.shape[0] == 1
    row = lambda a: a.reshape(1, -1).astype(F32)
    tm = 256 if t % 256 == 0 else CHUNK
    x2 = x.reshape(b * t, d)

    kv = _mem_kv(mem, row(norm_mem[0]), xattn_wkv[0].astype(BF16))

    gbias = jnp.pad(jnp.concatenate([mlstm_i_bias[0], mlstm_f_bias[0]]), (0, 128 - 2 * MLSTM_HEADS))
    zeros = jnp.zeros((DECAY_LORA, D_RWKV), F32)
    wa = jnp.concatenate([jnp.concatenate([rwkv_w_up[0], zeros], axis=1),
                          jnp.concatenate([zeros, rwkv_a_up[0]], axis=1)], axis=0).astype(BF16)
    head_id = jnp.arange(D_RWKV // 2) // RWKV_HEAD_DIM
    seg = (head_id[:, None] == head_id[None, :]).astype(BF16)
    y = _mixer(x, row(norm_mix[0]), _regroup_w_in(w_in[0]), mlstm_conv[0], row(gbias), row(mlstm_norm[0]),
               row(rwkv_mu[0]), row(rwkv_w0[0]), row(rwkv_a0[0]), wa, rwkv_g_up[0].astype(BF16),
               row(rwkv_k_k[0]), row(rwkv_k_a[0]), row(rwkv_r_k[0]), row(rwkv_ln_w[0]), row(rwkv_ln_b[0]),
               seg, _chunk_masks())

    out = _tail(x2, y.reshape(b * t, d), kv, w_mix_out[0].astype(BF16), row(norm_xattn[0]),
                xattn_wq[0].astype(BF16), xattn_wo[0].astype(BF16), row(norm_ffn[0]),
                ffn_w_gate[0].astype(BF16), ffn_w_up[0].astype(BF16), ffn_w_down[0].astype(BF16),
                row(norm_final), tm, t)
    return out.reshape(b, t, d)
```

```python
import functools

import jax
import jax.numpy as jnp
from jax import lax
from jax.experimental import pallas as pl
from jax.experimental.pallas import tpu as pltpu

F32 = jnp.float32
BF16 = jnp.bfloat16

EPS = 1e-6
RWKV_LN_EPS = 64e-5
CONV_WIDTH = 4

D_MODEL = 1024
D_MLSTM = 512
D_RWKV = 512
MLSTM_HEADS = 4
MLSTM_HEAD_DIM = 128
RWKV_HEAD_DIM = 64
RWKV_PAIRS = D_RWKV // (2 * RWKV_HEAD_DIM)
DECAY_LORA = 64
ICLR_LORA = 64
GATE_LORA = 128
XATTN_HEADS = 4
XATTN_HEAD_DIM = 256
CHUNK = 128
INV_BASE = 16
PROJ_COLS = 512
STREAMS = 2
STREAM_STAGGER = 0
FF_COLS = 256
TAIL_ROWS = 512

MASK_LOWER, MASK_STRICT = 0, 1
MASK_SAME = {16: 2, 32: 3, 64: 4}
MASK_OFF = {16: 5, 32: 6, 64: 7}

C_QK = 0
C_MV = C_QK + 2 * D_MLSTM
C_MO = C_MV + D_MLSTM
C_MG = C_MO + D_MLSTM
C_RW = C_MG + 128
D_RW_IN = 3 * D_RWKV + DECAY_LORA + ICLR_LORA + GATE_LORA
C_END = C_RW + D_RW_IN

VMEM_LIMIT = 56 * 1024 * 1024


def _bdot(a, b):
    return jnp.dot(a.astype(BF16), b.astype(BF16), preferred_element_type=F32)


def _bdot_nt(a, b):
    return lax.dot_general(a.astype(BF16), b.astype(BF16), (((1,), (1,)), ((), ())),
                           preferred_element_type=F32)


def _bdot_tn(a, b):
    return lax.dot_general(a.astype(BF16), b.astype(BF16), (((0,), (0,)), ((), ())),
                           preferred_element_type=F32)


def _split3(x):
    hi = x.astype(BF16)
    r1 = x - hi.astype(F32)
    mid = r1.astype(BF16)
    lo = (r1 - mid.astype(F32)).astype(BF16)
    return hi, mid, lo


def _exact_left_dot(ones_mat, x):
    hi, mid, lo = _split3(x)
    acc = jnp.dot(ones_mat, hi, preferred_element_type=F32)
    acc += jnp.dot(ones_mat, mid, preferred_element_type=F32)
    acc += jnp.dot(ones_mat, lo, preferred_element_type=F32)
    return acc


def _head_sums(x, seg):
    n, w = x.shape
    hi = x.astype(BF16)
    lo = (x - hi.astype(F32)).astype(BF16)
    stacked = jnp.concatenate([hi[:, :w // 2], hi[:, w // 2:], lo[:, :w // 2], lo[:, w // 2:]], axis=0)
    sums = jnp.dot(stacked, seg, preferred_element_type=F32)
    return jnp.concatenate([sums[0:n] + sums[2 * n:3 * n], sums[n:2 * n] + sums[3 * n:4 * n]], axis=1)


def _rms(x, g):
    return x * lax.rsqrt(jnp.mean(x * x, axis=-1, keepdims=True) + EPS) * g


def _sigmoid(x):
    return 1.0 / (1.0 + jnp.exp(-x))


def _log_sigmoid(x):
    return jnp.minimum(x, 0.0) - jnp.log(1.0 + jnp.exp(-jnp.abs(x)))


def _softplus(x):
    return jnp.maximum(x, 0.0) + jnp.log(1.0 + jnp.exp(-jnp.abs(x)))


def _mem_kv_kernel(m_ref, g_ref, w_ref, kv_ref):
    u = _rms(m_ref[0], g_ref[...]).astype(BF16)
    kv_ref[0] = jnp.dot(u, w_ref[...], preferred_element_type=F32).astype(BF16)


def _mem_kv(mem, g, w):
    b, m, d = mem.shape
    c = w.shape[1]
    return pl.pallas_call(
        _mem_kv_kernel,
        grid=(b,),
        in_specs=[pl.BlockSpec((1, m, d), lambda i: (i, 0, 0)),
                  pl.BlockSpec((1, d), lambda i: (0, 0)),
                  pl.BlockSpec((d, c), lambda i: (0, 0), pipeline_mode=pl.Buffered(1))],
        out_specs=pl.BlockSpec((1, m, c), lambda i: (i, 0, 0)),
        out_shape=jax.ShapeDtypeStruct((b, m, c), BF16),
        compiler_params=pltpu.CompilerParams(dimension_semantics=("parallel",),
                                             vmem_limit_bytes=VMEM_LIMIT),
        name="mem_kv",
    )(mem, g, w)


def _interleave(*gens):
    results = [None] * len(gens)
    live = list(enumerate(gens))
    while live:
        still = []
        for i, g in live:
            try:
                next(g)
                still.append((i, g))
            except StopIteration as stop:
                results[i] = stop.value
        live = still
    return results


def _interleaved(*gens):
    results = [None] * len(gens)
    live = list(enumerate(gens))
    while live:
        still = []
        for i, g in live:
            try:
                next(g)
                still.append((i, g))
            except StopIteration as stop:
                results[i] = stop.value
        live = still
        yield
    return results


def _delayed(gen, steps):
    for _ in range(steps):
        yield
    return (yield from gen)


def _fold_rows(x, s):
    acc = x[0:s]
    for j in range(1, x.shape[0] // s):
        acc = acc + x[j * s:(j + 1) * s]
    return acc


def _tile_rows(xc, n):
    return jnp.concatenate([xc] * (n // xc.shape[0]), axis=0)


def _unit_lower_inverses(ms, same_blk, off_blk):
    n = ms[0].shape[0]
    b = INV_BASE
    dot = functools.partial(jnp.dot, preferred_element_type=F32)
    lane = lax.broadcasted_iota(jnp.int32, (b, 2 * n), 1)
    eye_c = (lane % b == lax.broadcasted_iota(jnp.int32, (b, 2 * n), 0)).astype(F32)
    zeros = jnp.zeros((n, n), BF16)
    pairs = [(ms[j], ms[j + 1]) for j in range(0, len(ms), 2)]

    def block_diag2(lo, hi):
        return jnp.concatenate([jnp.concatenate([lo, zeros], axis=1),
                                jnp.concatenate([zeros, hi], axis=1)], axis=0)

    def fold2(pair, mask, s):
        return jnp.concatenate([_fold_rows(m * mask, s) for m in pair], axis=1)

    def expand2(xc, mask):
        return block_diag2(_tile_rows(xc[:, :n].astype(BF16), n) * mask,
                           _tile_rows(xc[:, n:].astype(BF16), n) * mask)

    m_diag = [(lo * same_blk[b], hi * same_blk[b]) for lo, hi in pairs]
    pcs = [jnp.concatenate([_fold_rows(lo, b), _fold_rows(hi, b)], axis=1) for lo, hi in m_diag]
    tcs = [eye_c + p.astype(F32) for p in pcs]
    pcs = [dot(p, block_diag2(lo, hi)) for p, (lo, hi) in zip(pcs, m_diag)]
    yield
    s = 2
    while 2 * s < b:
        xs = [dot(jnp.concatenate([t, p], axis=0).astype(BF16), expand2(p, same_blk[b]))
              for t, p in zip(tcs, pcs)]
        tcs = [t + x[:b] for t, x in zip(tcs, xs)]
        pcs = [x[b:] for x in xs]
        s *= 2
        yield
    tcs = [t + dot(t.astype(BF16), expand2(p, same_blk[b])) for t, p in zip(tcs, pcs)]
    yield
    s = b
    while s < n:
        ccs = [fold2(pr, off_blk[s], s) for pr in pairs]
        x1s = [dot(c, expand2(t, same_blk[s])) for c, t in zip(ccs, tcs)]
        yield
        x2s = [dot(t.astype(BF16), expand2(x, off_blk[s])) for t, x in zip(tcs, x1s)]
        even = (lax.broadcasted_iota(jnp.int32, (s, 2 * n), 1) // s) % 2 == 0
        tcs = [jnp.concatenate([jnp.where(even, t, 0.0), x + jnp.where(even, 0.0, t)], axis=0)
               for t, x in zip(tcs, x2s)]
        yield
        s *= 2
    return tcs


def _mlstm_chunk(zvo, qkbuf, conv_ref, gbias_ref, mnorm_ref, states, tri, lower):
    L = CHUNK
    conv_w = conv_ref[...]
    qk = qkbuf[8:8 + L, :] * conv_w[CONV_WIDTH - 1:CONV_WIDTH, :]
    for j in range(1, CONV_WIDTH):
        qk += qkbuf[8 - j:8 - j + L, :] * conv_w[CONV_WIDTH - 1 - j:CONV_WIDTH - j, :]
    yield
    qk = qk * _sigmoid(qk)
    q_all = qk[:, :D_MLSTM]
    k_all = qk[:, D_MLSTM:] * (MLSTM_HEAD_DIM ** -0.5)
    yield

    gates = zvo[:, C_MG - C_MV:C_MG - C_MV + 128] + gbias_ref[...]
    bcum = _exact_left_dot(tri, _log_sigmoid(gates))
    gates_t = gates.T
    bcum_t = bcum.T

    outs, new_states = [], []
    for h in range(MLSTM_HEADS):
        hs = slice(h * MLSTM_HEAD_DIM, (h + 1) * MLSTM_HEAD_DIM)
        qh = q_all[:, hs].astype(BF16)
        kh = k_all[:, hs]
        vh = zvo[:, h * MLSTM_HEAD_DIM:(h + 1) * MLSTM_HEAD_DIM]
        oh = zvo[:, C_MO - C_MV + h * MLSTM_HEAD_DIM:C_MO - C_MV + (h + 1) * MLSTM_HEAD_DIM]
        f_lane = MLSTM_HEADS + h
        b_col = bcum[:, f_lane:f_lane + 1]
        b_row = bcum_t[f_lane:f_lane + 1, :]
        ig_col = gates[:, h:h + 1]
        ig_row = gates_t[h:h + 1, :]
        g_tot = bcum[L - 1:L, f_lane:f_lane + 1]
        ct_prev, n_prev, m_prev = states[h]
        m_prev = m_prev[:, 0:1]

        dm = jnp.where(lower, b_col - b_row + ig_row, -jnp.inf)
        m_intra = jnp.max(dm, axis=-1, keepdims=True)
        m_inter = b_col + m_prev
        m_t = jnp.maximum(m_inter, m_intra)
        s_mat = _bdot_nt(qh, kh) * jnp.exp(dm - m_t)
        s_inter = jnp.exp(m_inter - m_t)
        num = _bdot(s_mat, vh) + s_inter * _bdot(qh, ct_prev)
        den = (jnp.sum(s_mat, axis=-1, keepdims=True)
               + s_inter * jnp.sum(q_all[:, hs] * n_prev, axis=-1, keepdims=True))
        hh = num / jnp.maximum(jnp.abs(den), jnp.exp(-m_t))
        yield

        a_col = g_tot - b_col + ig_col
        m_loc = jnp.max(a_col, axis=0, keepdims=True)
        wgt = jnp.exp(a_col - m_loc)
        d_ct = _bdot_tn(kh, vh * wgt)
        d_n = jnp.sum(kh * wgt, axis=0, keepdims=True)
        m_new = jnp.maximum(g_tot + m_prev, m_loc)
        s_old = jnp.exp(g_tot + m_prev - m_new)
        s_new = jnp.exp(m_loc - m_new)
        new_states.append((s_old * ct_prev + s_new * d_ct, s_old * n_prev + s_new * d_n,
                           jnp.broadcast_to(m_new, (1, 128))))

        hh = hh * lax.rsqrt(jnp.mean(hh * hh, axis=-1, keepdims=True) + EPS)
        outs.append(hh * mnorm_ref[:, hs] * _sigmoid(oh))
        yield
    return outs, new_states


def _project_next(x_ref, g_ref, w_ref, znext):
    u = _rms(x_ref[...].reshape(STREAMS * CHUNK, D_MODEL), g_ref[...]).astype(BF16)
    yield
    for c0 in range(0, C_END, PROJ_COLS):
        c1 = min(c0 + PROJ_COLS, C_END)
        znext[:, c0:c1] = jnp.dot(u, w_ref[:, c0:c1], preferred_element_type=F32)
        yield


def _chunk_stream(zvo, qkbuf, rwbuf, s_prev, mlstm_states, prm, masks_ref):
    L = CHUNK
    pairs = range(RWKV_PAIRS)
    tri = masks_ref[MASK_LOWER]
    lower = tri > 0
    strict = masks_ref[MASK_STRICT] > 0
    same_blk = {s: masks_ref[MASK_SAME[s]] for s in MASK_SAME}
    off_blk = {s: masks_ref[MASK_OFF[s]] for s in MASK_OFF}

    z_now = rwbuf[8:8 + L, :]
    z_prev = rwbuf[7:7 + L, :]
    zm = z_now + (z_prev - z_now) * prm["mu"][...]
    r = zm[:, 0:D_RWKV]
    k = zm[:, D_RWKV:2 * D_RWKV]
    v = zm[:, 2 * D_RWKV:3 * D_RWKV]
    xwa = zm[:, 3 * D_RWKV:3 * D_RWKV + 128]
    xg = zm[:, 3 * D_RWKV + 128:3 * D_RWKV + 256]
    lane128 = lax.broadcasted_iota(jnp.int32, (L, 128), 1)
    lora_in = jnp.where(lane128 < DECAY_LORA, jnp.tanh(xwa), xwa)
    yield
    lora = _bdot(lora_in, prm["wa"][...])
    w_raw = -_softplus(-(prm["w0"][...] + lora[:, :D_RWKV])) - 0.5
    logw = -jnp.exp(w_raw)
    iclr = _sigmoid(prm["a0"][...] + lora[:, D_RWKV:])
    gate = _bdot(_sigmoid(xg), prm["gup"][...])
    yield

    seg = prm["seg"][...]
    kk = k * prm["kk"][...]
    k2 = k * (1.0 + (iclr - 1.0) * prm["ka"][...])
    seg_sums = _head_sums(jnp.concatenate([kk * kk, r * k2 * prm["rk"][...]], axis=0), seg)
    kk = kk / jnp.maximum(jnp.sqrt(seg_sums[:L]), 1e-12)
    bonus = seg_sums[L:]
    a_vec = -kk
    b_vec = kk * iclr
    yield

    cum = _exact_left_dot(tri, logw)
    c_ref = cum[L // 2 - 1:L // 2, :]
    c_end = cum[L - 1:L, :]
    e_pos = jnp.exp(cum - c_ref)
    e_neg = jnp.exp(c_ref - cum)
    r_rel = r * e_pos
    a_rel = a_vec * jnp.exp(-logw) * e_pos
    k_rel = k2 * e_neg
    b_rel = b_vec * e_neg
    e_ref = jnp.exp(c_ref)
    e_tail = jnp.exp(c_end - c_ref)
    e_end = jnp.exp(c_end)
    yield

    lane_lo = lane128 < RWKV_HEAD_DIM
    lane_lo2 = lax.broadcasted_iota(jnp.int32, (2 * L, 128), 1) < RWKV_HEAD_DIM
    blockdiag = same_blk[RWKV_HEAD_DIM].astype(F32)

    pair_slices = [slice(p * 128, (p + 1) * 128) for p in pairs]
    m_ab, m_ak, m_r = [], [], []
    ar_rel, bk_rel = [], []
    zero = jnp.zeros((), BF16)
    for ps in pair_slices:
        ar_rel.append(jnp.concatenate([a_rel[:, ps], r_rel[:, ps]], axis=0).astype(BF16))
        bk_rel.append(jnp.concatenate([b_rel[:, ps], k_rel[:, ps]], axis=0).astype(BF16))
        ar_split = jnp.concatenate([jnp.where(lane_lo2, ar_rel[-1], zero),
                                    jnp.where(lane_lo2, zero, ar_rel[-1])], axis=0)
        g_both = _bdot_nt(ar_split, bk_rel[-1]).astype(BF16)
        for half in range(2):
            g_mat = g_both[2 * L * half:2 * L * (half + 1)]
            m_ab.append(jnp.where(strict, g_mat[:L, :L], zero))
            m_ak.append(jnp.where(strict, g_mat[:L, L:], zero))
            m_r.append(jnp.concatenate([jnp.where(lower, g_mat[L:, :L], zero),
                                        jnp.where(lower, g_mat[L:, L:], zero)], axis=1))
    yield
    t_inv, (m_outs, m_new) = yield from _interleaved(
        _unit_lower_inverses(m_ab, same_blk, off_blk),
        _mlstm_chunk(zvo, qkbuf, prm["conv"], prm["gbias"], prm["mnorm"], mlstm_states, tri, lower))

    def both_heads(stacked):
        return jnp.where(lane_lo, stacked[:L], stacked[L:])

    def two_pairs(lhs0, lhs1, rhs0, rhs1, rhs_transposed=False):
        z = jnp.zeros((128, 128), BF16)
        rhs = jnp.concatenate([jnp.concatenate([rhs0.astype(BF16), z], axis=1),
                               jnp.concatenate([z, rhs1.astype(BF16)], axis=1)], axis=0)
        lhs = jnp.concatenate([lhs0.astype(BF16), lhs1.astype(BF16)], axis=1)
        out = _bdot_nt(lhs, rhs) if rhs_transposed else jnp.dot(lhs, rhs, preferred_element_type=F32)
        return out[:, :128], out[:, 128:]

    v_ps = [v[:, ps].astype(BF16) for ps in pair_slices]
    s_at_ref = [s_prev[p] * e_ref[:, ps] for p, ps in enumerate(pair_slices)]
    inter, rhs = [None] * RWKV_PAIRS, [None] * RWKV_PAIRS
    for p in range(0, RWKV_PAIRS, 2):
        inter[p], inter[p + 1] = two_pairs(ar_rel[p], ar_rel[p + 1], s_at_ref[p], s_at_ref[p + 1],
                                           rhs_transposed=True)
        mv0, mv1 = two_pairs(jnp.concatenate([m_ak[2 * p], m_ak[2 * p + 1]], axis=0),
                             jnp.concatenate([m_ak[2 * p + 2], m_ak[2 * p + 3]], axis=0), v_ps[p], v_ps[p + 1])
        rhs[p] = inter[p][:L] + both_heads(mv0)
        rhs[p + 1] = inter[p + 1][:L] + both_heads(mv1)
    yield
    u_ps = [None] * RWKV_PAIRS
    for p in range(0, RWKV_PAIRS, 2):
        t0, t1 = t_inv[p], t_inv[p + 1]
        u0, u1 = two_pairs(jnp.concatenate([t0[:, :L], t0[:, L:]], axis=0),
                           jnp.concatenate([t1[:, :L], t1[:, L:]], axis=0), rhs[p], rhs[p + 1])
        u_ps[p], u_ps[p + 1] = both_heads(u0), both_heads(u1)
    uv = [jnp.concatenate([u_ps[p].astype(BF16), v_ps[p]], axis=0) for p in pairs]
    yield
    y_parts = [inter[p][L:] + both_heads(_bdot(jnp.concatenate([m_r[2 * p], m_r[2 * p + 1]], axis=0), uv[p]))
               for p in pairs]
    s_new = [s_prev[p] * e_end[:, ps] + (blockdiag * e_tail[:, ps]) * _bdot_tn(uv[p], bk_rel[p])
             for p, ps in enumerate(pair_slices)]
    yield

    y = jnp.concatenate(y_parts, axis=1)
    inv_n = 1.0 / RWKV_HEAD_DIM
    mean = _head_sums(y, seg) * inv_n
    yc = y - mean
    var = _head_sums(yc * yc, seg) * inv_n
    y = yc * lax.rsqrt(var + RWKV_LN_EPS) * prm["lnw"][...] + prm["lnb"][...]
    y = (y + bonus * v) * gate
    return m_outs, m_new, y, s_new


MIXER_PARAMS = ("conv", "gbias", "mnorm", "mu", "w0", "a0", "wa", "gup", "kk", "ka", "rk", "lnw", "lnb", "seg")


def _mixer_kernel(xn_ref, x0_ref, gin_ref, win_ref, *rest):
    prm = dict(zip(MIXER_PARAMS, rest[:len(MIXER_PARAMS)]))
    masks_ref, y_ref, znext, zvo, qkbuf, rwbuf, ct_ref, n_ref, m_ref, s_ref = rest[len(MIXER_PARAMS):]
    L = CHUNK
    t_idx = pl.program_id(1)

    @pl.when(t_idx == 0)
    def _():
        u0 = _rms(x0_ref[...].reshape(STREAMS * L, D_MODEL), gin_ref[...]).astype(BF16)
        znext[...] = jnp.dot(u0, win_ref[...], preferred_element_type=F32)
        qkbuf[...] = jnp.zeros_like(qkbuf)
        rwbuf[...] = jnp.zeros_like(rwbuf)
        ct_ref[...] = jnp.zeros_like(ct_ref)
        n_ref[...] = jnp.zeros_like(n_ref)
        m_ref[...] = jnp.zeros_like(m_ref)
        s_ref[...] = jnp.zeros_like(s_ref)

    @pl.when(t_idx > 0)
    def _():
        for q in range(STREAMS):
            qkbuf[q, 0:8, :] = qkbuf[q, L:L + 8, :]
            rwbuf[q, 0:8, :] = rwbuf[q, L:L + 8, :]

    for q in range(STREAMS):
        rows = slice(q * L, (q + 1) * L)
        qkbuf[q, 8:8 + L, :] = znext[rows, C_QK:C_QK + 2 * D_MLSTM]
        zvo[q] = znext[rows, C_MV:C_RW]
        rwbuf[q, 8:8 + L, :] = znext[rows, C_RW:C_END]

    streams = []
    for q in range(STREAMS):
        s_prev = [s_ref[q, p] for p in range(RWKV_PAIRS)]
        mlstm_states = [(ct_ref[q, h], n_ref[q, h], m_ref[q, h]) for h in range(MLSTM_HEADS)]
        stream = _chunk_stream(zvo.at[q], qkbuf.at[q], rwbuf.at[q], s_prev, mlstm_states, prm, masks_ref)
        streams.append(_delayed(stream, q * STREAM_STAGGER))

    *results, _ = _interleave(*streams, _project_next(xn_ref, gin_ref, win_ref, znext))

    for q, (m_outs, m_new, y, s_new) in enumerate(results):
        for h in range(MLSTM_HEADS):
            y_ref[q, :, h * MLSTM_HEAD_DIM:(h + 1) * MLSTM_HEAD_DIM] = m_outs[h].astype(y_ref.dtype)
            ct_ref[q, h], n_ref[q, h], m_ref[q, h] = m_new[h]
        y_ref[q, :, D_MLSTM:] = y.astype(y_ref.dtype)
        for p in range(RWKV_PAIRS):
            s_ref[q, p] = s_new[p]


def _mixer(x, gin, w_in, mixer_params, masks):
    b, t, d = x.shape
    nt = t // CHUNK
    const = lambda a: pl.BlockSpec(a.shape, lambda i, j: (0,) * a.ndim, pipeline_mode=pl.Buffered(1))
    params = (gin, w_in) + tuple(mixer_params) + (masks,)
    blk = (STREAMS, CHUNK, d)
    return pl.pallas_call(
        _mixer_kernel,
        grid=(b // STREAMS, nt),
        in_specs=[pl.BlockSpec(blk, lambda i, j: (i, jnp.minimum(j + 1, nt - 1), 0)),
                  pl.BlockSpec(blk, lambda i, j: (i, 0, 0))] + [const(a) for a in params],
        out_specs=pl.BlockSpec(blk, lambda i, j: (i, j, 0)),
        out_shape=jax.ShapeDtypeStruct((b, t, d), BF16),
        scratch_shapes=[
            pltpu.VMEM((STREAMS * CHUNK, C_END), F32),
            pltpu.VMEM((STREAMS, CHUNK, C_RW - C_MV), F32),
            pltpu.VMEM((STREAMS, CHUNK + 8, 2 * D_MLSTM), F32),
            pltpu.VMEM((STREAMS, CHUNK + 8, D_RW_IN), F32),
            pltpu.VMEM((STREAMS, MLSTM_HEADS, MLSTM_HEAD_DIM, MLSTM_HEAD_DIM), F32),
            pltpu.VMEM((STREAMS, MLSTM_HEADS, 1, MLSTM_HEAD_DIM), F32),
            pltpu.VMEM((STREAMS, MLSTM_HEADS, 1, 128), F32),
            pltpu.VMEM((STREAMS, RWKV_PAIRS, 128, 128), F32),
        ],
        compiler_params=pltpu.CompilerParams(dimension_semantics=("parallel", "arbitrary"),
                                             vmem_limit_bytes=VMEM_LIMIT),
        name="mixer",
    )(x, x, *params)


def _tail_kernel(x_ref, y_ref, kv_ref, wmix_ref, gx_ref, wq_ref, wo_ref, gf_ref, wg_ref, wu_ref, wd_ref,
                 gfin_ref, o_ref):
    h = x_ref[...] + jnp.dot(y_ref[...], wmix_ref[...], preferred_element_type=F32)

    q = _bdot(_rms(h, gx_ref[...]), wq_ref[...]).astype(BF16)
    heads = []
    for i in range(XATTN_HEADS):
        hs = slice(i * XATTN_HEAD_DIM, (i + 1) * XATTN_HEAD_DIM)
        k_h = kv_ref[0, :, i * XATTN_HEAD_DIM:(i + 1) * XATTN_HEAD_DIM]
        v_h = kv_ref[0, :, D_MODEL + i * XATTN_HEAD_DIM:D_MODEL + (i + 1) * XATTN_HEAD_DIM]
        s = _bdot_nt(q[:, hs], k_h) * (XATTN_HEAD_DIM ** -0.5)
        e = jnp.exp(s - jnp.max(s, axis=-1, keepdims=True))
        heads.append((_bdot(e, v_h) / jnp.sum(e, axis=-1, keepdims=True)).astype(BF16))
    h = h + jnp.dot(jnp.concatenate(heads, axis=1), wo_ref[...], preferred_element_type=F32)

    u = _rms(h, gf_ref[...]).astype(BF16)
    for c0 in range(0, wg_ref.shape[1], FF_COLS):
        gate = jnp.dot(u, wg_ref[:, c0:c0 + FF_COLS], preferred_element_type=F32)
        up = jnp.dot(u, wu_ref[:, c0:c0 + FF_COLS], preferred_element_type=F32)
        act = (gate * _sigmoid(gate) * up).astype(BF16)
        h = h + jnp.dot(act, wd_ref[c0:c0 + FF_COLS, :], preferred_element_type=F32)

    o_ref[...] = _rms(h, gfin_ref[...])


def _tail(x2, y2, kv, tail_params, tm, t):
    n, d = x2.shape
    per_b = t // tm
    const = lambda a: pl.BlockSpec(a.shape, lambda i: (0,) * a.ndim, pipeline_mode=pl.Buffered(1))
    return pl.pallas_call(
        _tail_kernel,
        grid=(n // tm,),
        in_specs=[pl.BlockSpec((tm, d), lambda i: (i, 0)),
                  pl.BlockSpec((tm, d), lambda i: (i, 0)),
                  pl.BlockSpec((1,) + kv.shape[1:], lambda i: (i // per_b, 0, 0))]
                 + [const(a) for a in tail_params],
        out_specs=pl.BlockSpec((tm, d), lambda i: (i, 0)),
        out_shape=jax.ShapeDtypeStruct((n, d), F32),
        compiler_params=pltpu.CompilerParams(dimension_semantics=("parallel",),
                                             vmem_limit_bytes=VMEM_LIMIT),
        name="tail",
    )(x2, y2, kv, *tail_params)


def _chunk_masks():
    i = jnp.arange(CHUNK)
    row, col = i[:, None], i[None, :]
    planes = [None] * 8
    planes[MASK_LOWER] = row >= col
    planes[MASK_STRICT] = row > col
    for s, idx in MASK_SAME.items():
        planes[idx] = (row // s) == (col // s)
    for s, idx in MASK_OFF.items():
        planes[idx] = ((row // (2 * s)) == (col // (2 * s))) & ((row // s) > (col // s))
    return jnp.stack(planes).astype(BF16)


def _regroup_w_in(w):
    n_m = 4 * D_MLSTM
    gates = w[:, n_m:n_m + 2 * MLSTM_HEADS]
    gates = jnp.pad(gates, ((0, 0), (0, 128 - 2 * MLSTM_HEADS)))
    return jnp.concatenate([w[:, :n_m], gates, w[:, n_m + 2 * MLSTM_HEADS:]], axis=1).astype(BF16)


def kernel(x, mem, norm_mix, w_in, mlstm_conv, mlstm_i_bias, mlstm_f_bias, mlstm_norm, rwkv_mu, rwkv_w0, rwkv_w_up, rwkv_a0, rwkv_a_up, rwkv_g_up, rwkv_k_k, rwkv_k_a, rwkv_r_k, rwkv_ln_w, rwkv_ln_b, w_mix_out, norm_xattn, norm_mem, xattn_wq, xattn_wkv, xattn_wo, norm_ffn, ffn_w_gate, ffn_w_up, ffn_w_down, norm_final):
    b, t, d = x.shape
    assert d == D_MODEL and t % CHUNK == 0 and b % STREAMS == 0 and norm_mix.shape[0] == 1
    row = lambda a: a.reshape(1, -1).astype(F32)

    kv = _mem_kv(mem, row(norm_mem[0]), xattn_wkv[0].astype(BF16))

    gbias = jnp.pad(jnp.concatenate([mlstm_i_bias[0], mlstm_f_bias[0]]), (0, 128 - 2 * MLSTM_HEADS))
    zeros = jnp.zeros((DECAY_LORA, D_RWKV), F32)
    wa = jnp.concatenate([jnp.concatenate([rwkv_w_up[0], zeros], axis=1),
                          jnp.concatenate([zeros, rwkv_a_up[0]], axis=1)], axis=0).astype(BF16)
    head_id = jnp.arange(D_RWKV // 2) // RWKV_HEAD_DIM
    seg = (head_id[:, None] == head_id[None, :]).astype(BF16)
    mixer_params = (mlstm_conv[0], row(gbias), row(mlstm_norm[0]), row(rwkv_mu[0]), row(rwkv_w0[0]),
                    row(rwkv_a0[0]), wa, rwkv_g_up[0].astype(BF16), row(rwkv_k_k[0]), row(rwkv_k_a[0]),
                    row(rwkv_r_k[0]), row(rwkv_ln_w[0]), row(rwkv_ln_b[0]), seg)
    tail_params = (w_mix_out[0].astype(BF16), row(norm_xattn[0]), xattn_wq[0].astype(BF16),
                   xattn_wo[0].astype(BF16), row(norm_ffn[0]), ffn_w_gate[0].astype(BF16),
                   ffn_w_up[0].astype(BF16), ffn_w_down[0].astype(BF16), row(norm_final))
    y = _mixer(x, row(norm_mix[0]), _regroup_w_in(w_in[0]), mixer_params, _chunk_masks())
    tm = TAIL_ROWS if t % TAIL_ROWS == 0 else CHUNK
    out = _tail(x.reshape(b * t, d), y.reshape(b * t, d), kv, tail_params, tm, t)
    return out.reshape(b, t, d)
```

```python
import functools

import jax
import jax.numpy as jnp
from jax import lax
from jax.experimental import pallas as pl
from jax.experimental.pallas import tpu as pltpu

F32 = jnp.float32
BF16 = jnp.bfloat16

EPS = 1e-6
RWKV_LN_EPS = 64e-5
CONV_WIDTH = 4

D_MODEL = 1024
D_MLSTM = 512
D_RWKV = 512
MLSTM_HEADS = 4
MLSTM_HEAD_DIM = 128
RWKV_HEAD_DIM = 64
RWKV_PAIRS = D_RWKV // (2 * RWKV_HEAD_DIM)
DECAY_LORA = 64
ICLR_LORA = 64
GATE_LORA = 128
XATTN_HEADS = 4
XATTN_HEAD_DIM = 256
CHUNK = 128
INV_BASE = 16
PROJ_COLS = 256
STREAMS = 2
FF_COLS = 256
TAIL_ROWS = 512

MASK_LOWER, MASK_STRICT = 0, 1
MASK_SAME = {16: 2, 32: 3, 64: 4}
MASK_OFF = {16: 5, 32: 6, 64: 7}

C_QK = 0
C_MV = C_QK + 2 * D_MLSTM
C_MO = C_MV + D_MLSTM
C_MG = C_MO + D_MLSTM
C_RW = C_MG + 128
D_RW_IN = 3 * D_RWKV + DECAY_LORA + ICLR_LORA + GATE_LORA
C_END = C_RW + D_RW_IN

VMEM_LIMIT = 56 * 1024 * 1024


def _bdot(a, b):
    return jnp.dot(a.astype(BF16), b.astype(BF16), preferred_element_type=F32)


def _bdot_nt(a, b):
    return lax.dot_general(a.astype(BF16), b.astype(BF16), (((1,), (1,)), ((), ())),
                           preferred_element_type=F32)


def _bdot_tn(a, b):
    return lax.dot_general(a.astype(BF16), b.astype(BF16), (((0,), (0,)), ((), ())),
                           preferred_element_type=F32)


def _split3(x):
    hi = x.astype(BF16)
    r1 = x - hi.astype(F32)
    mid = r1.astype(BF16)
    lo = (r1 - mid.astype(F32)).astype(BF16)
    return hi, mid, lo


def _exact_left_dot(ones_mat, x):
    hi, mid, lo = _split3(x)
    acc = jnp.dot(ones_mat, hi, preferred_element_type=F32)
    acc += jnp.dot(ones_mat, mid, preferred_element_type=F32)
    acc += jnp.dot(ones_mat, lo, preferred_element_type=F32)
    return acc


def _head_sums(x, seg):
    n, w = x.shape
    hi = x.astype(BF16)
    lo = (x - hi.astype(F32)).astype(BF16)
    stacked = jnp.concatenate([hi[:, :w // 2], hi[:, w // 2:], lo[:, :w // 2], lo[:, w // 2:]], axis=0)
    sums = jnp.dot(stacked, seg, preferred_element_type=F32)
    return jnp.concatenate([sums[0:n] + sums[2 * n:3 * n], sums[n:2 * n] + sums[3 * n:4 * n]], axis=1)


def _rms(x, g):
    return x * lax.rsqrt(jnp.mean(x * x, axis=-1, keepdims=True) + EPS) * g


def _sigmoid(x):
    return 1.0 / (1.0 + jnp.exp(-x))


def _log_sigmoid(x):
    return jnp.minimum(x, 0.0) - jnp.log(1.0 + jnp.exp(-jnp.abs(x)))


def _softplus(x):
    return jnp.maximum(x, 0.0) + jnp.log(1.0 + jnp.exp(-jnp.abs(x)))


def _mem_kv_kernel(m_ref, g_ref, w_ref, kv_ref):
    u = _rms(m_ref[0], g_ref[...]).astype(BF16)
    kv_ref[0] = jnp.dot(u, w_ref[...], preferred_element_type=F32).astype(BF16)


def _mem_kv(mem, g, w):
    b, m, d = mem.shape
    c = w.shape[1]
    return pl.pallas_call(
        _mem_kv_kernel,
        grid=(b,),
        in_specs=[pl.BlockSpec((1, m, d), lambda i: (i, 0, 0)),
                  pl.BlockSpec((1, d), lambda i: (0, 0)),
                  pl.BlockSpec((d, c), lambda i: (0, 0), pipeline_mode=pl.Buffered(1))],
        out_specs=pl.BlockSpec((1, m, c), lambda i: (i, 0, 0)),
        out_shape=jax.ShapeDtypeStruct((b, m, c), BF16),
        compiler_params=pltpu.CompilerParams(dimension_semantics=("parallel",),
                                             vmem_limit_bytes=VMEM_LIMIT),
        name="mem_kv",
    )(mem, g, w)


def _interleaved(*gens, filler=None):
    results = [None] * len(gens)
    live = list(enumerate(gens))
    while live:
        still, hint = [], 0
        for i, g in live:
            try:
                hint = max(hint, next(g) or 0)
                still.append((i, g))
            except StopIteration as stop:
                results[i] = stop.value
        live = still
        if filler is not None:
            for _ in range(hint):
                next(filler, None)
        yield hint
    if filler is not None:
        for _ in filler:
            pass
    return results


def _run(gen):
    try:
        while True:
            next(gen)
    except StopIteration as stop:
        return stop.value


def _fold_rows(x, s):
    acc = x[0:s]
    for j in range(1, x.shape[0] // s):
        acc = acc + x[j * s:(j + 1) * s]
    return acc


def _tile_rows(xc, n):
    return jnp.concatenate([xc] * (n // xc.shape[0]), axis=0)


def _unit_lower_inverses(ms, same_blk, off_blk):
    n = ms[0].shape[0]
    b = INV_BASE
    dot = functools.partial(jnp.dot, preferred_element_type=F32)
    lane = lax.broadcasted_iota(jnp.int32, (b, 2 * n), 1)
    eye_c = (lane % b == lax.broadcasted_iota(jnp.int32, (b, 2 * n), 0)).astype(F32)
    zeros = jnp.zeros((n, n), BF16)
    pairs = [(ms[j], ms[j + 1]) for j in range(0, len(ms), 2)]

    def block_diag2(lo, hi):
        return jnp.concatenate([jnp.concatenate([lo, zeros], axis=1),
                                jnp.concatenate([zeros, hi], axis=1)], axis=0)

    def fold2(pair, mask, s):
        return jnp.concatenate([_fold_rows(m * mask, s) for m in pair], axis=1)

    def expand2(xc, mask):
        return block_diag2(_tile_rows(xc[:, :n].astype(BF16), n) * mask,
                           _tile_rows(xc[:, n:].astype(BF16), n) * mask)

    m_diag = [(lo * same_blk[b], hi * same_blk[b]) for lo, hi in pairs]
    pcs = [jnp.concatenate([_fold_rows(lo, b), _fold_rows(hi, b)], axis=1) for lo, hi in m_diag]
    tcs = [eye_c + p.astype(F32) for p in pcs]
    pcs = [dot(p, block_diag2(lo, hi)) for p, (lo, hi) in zip(pcs, m_diag)]
    yield
    s = 2
    while 2 * s < b:
        xs = [dot(jnp.concatenate([t, p], axis=0).astype(BF16), expand2(p, same_blk[b]))
              for t, p in zip(tcs, pcs)]
        tcs = [t + x[:b] for t, x in zip(tcs, xs)]
        pcs = [x[b:] for x in xs]
        s *= 2
        yield
    tcs = [t + dot(t.astype(BF16), expand2(p, same_blk[b])) for t, p in zip(tcs, pcs)]
    yield
    s = b
    while s < n:
        ccs = [fold2(pr, off_blk[s], s) for pr in pairs]
        x1s = [dot(c, expand2(t, same_blk[s])) for c, t in zip(ccs, tcs)]
        yield
        x2s = [dot(t.astype(BF16), expand2(x, off_blk[s])) for t, x in zip(tcs, x1s)]
        even = (lax.broadcasted_iota(jnp.int32, (s, 2 * n), 1) // s) % 2 == 0
        tcs = [jnp.concatenate([jnp.where(even, t, 0.0), x + jnp.where(even, 0.0, t)], axis=0)
               for t, x in zip(tcs, x2s)]
        yield
        s *= 2
    return tcs


def _mlstm_chunk(zvo, qkbuf, conv_ref, gbias_ref, mnorm_ref, states, tri, lower):
    L = CHUNK
    conv_w = conv_ref[...]
    qk = qkbuf[8:8 + L, :] * conv_w[CONV_WIDTH - 1:CONV_WIDTH, :]
    for j in range(1, CONV_WIDTH):
        qk += qkbuf[8 - j:8 - j + L, :] * conv_w[CONV_WIDTH - 1 - j:CONV_WIDTH - j, :]
    yield 2
    qk = qk * _sigmoid(qk)
    q_all = qk[:, :D_MLSTM]
    k_all = qk[:, D_MLSTM:] * (MLSTM_HEAD_DIM ** -0.5)
    yield 2

    gates = zvo[:, C_MG - C_MV:C_MG - C_MV + 128] + gbias_ref[...]
    bcum = _exact_left_dot(tri, _log_sigmoid(gates))
    gates_t = gates.T
    bcum_t = bcum.T

    outs, new_states = [], []
    for h in range(MLSTM_HEADS):
        hs = slice(h * MLSTM_HEAD_DIM, (h + 1) * MLSTM_HEAD_DIM)
        qh = q_all[:, hs].astype(BF16)
        kh = k_all[:, hs]
        vh = zvo[:, h * MLSTM_HEAD_DIM:(h + 1) * MLSTM_HEAD_DIM]
        oh = zvo[:, C_MO - C_MV + h * MLSTM_HEAD_DIM:C_MO - C_MV + (h + 1) * MLSTM_HEAD_DIM]
        f_lane = MLSTM_HEADS + h
        b_col = bcum[:, f_lane:f_lane + 1]
        b_row = bcum_t[f_lane:f_lane + 1, :]
        ig_col = gates[:, h:h + 1]
        ig_row = gates_t[h:h + 1, :]
        g_tot = bcum[L - 1:L, f_lane:f_lane + 1]
        ct_prev, n_prev, m_prev = states[h]
        m_prev = m_prev[:, 0:1]

        dm = jnp.where(lower, b_col - b_row + ig_row, -jnp.inf)
        m_intra = jnp.max(dm, axis=-1, keepdims=True)
        m_inter = b_col + m_prev
        m_t = jnp.maximum(m_inter, m_intra)
        s_mat = _bdot_nt(qh, kh) * jnp.exp(dm - m_t)
        s_inter = jnp.exp(m_inter - m_t)
        num = _bdot(s_mat, vh) + s_inter * _bdot(qh, ct_prev)
        den = (jnp.sum(s_mat, axis=-1, keepdims=True)
               + s_inter * jnp.sum(q_all[:, hs] * n_prev, axis=-1, keepdims=True))
        hh = num / jnp.maximum(jnp.abs(den), jnp.exp(-m_t))
        yield

        a_col = g_tot - b_col + ig_col
        m_loc = jnp.max(a_col, axis=0, keepdims=True)
        wgt = jnp.exp(a_col - m_loc)
        d_ct = _bdot_tn(kh, vh * wgt)
        d_n = jnp.sum(kh * wgt, axis=0, keepdims=True)
        m_new = jnp.maximum(g_tot + m_prev, m_loc)
        s_old = jnp.exp(g_tot + m_prev - m_new)
        s_new = jnp.exp(m_loc - m_new)
        new_states.append((s_old * ct_prev + s_new * d_ct, s_old * n_prev + s_new * d_n,
                           jnp.broadcast_to(m_new, (1, 128))))

        hh = hh * lax.rsqrt(jnp.mean(hh * hh, axis=-1, keepdims=True) + EPS)
        outs.append(hh * mnorm_ref[:, hs] * _sigmoid(oh))
        yield
    return outs, new_states


def _project_next(x_ref, g_ref, w_ref, znext):
    u = _rms(x_ref[...].reshape(STREAMS * CHUNK, D_MODEL), g_ref[...]).astype(BF16)
    yield
    for c0 in range(0, C_END, PROJ_COLS):
        c1 = min(c0 + PROJ_COLS, C_END)
        znext[:, c0:c1] = jnp.dot(u, w_ref[:, c0:c1], preferred_element_type=F32)
        yield


def _chunk_stream(zvo, qkbuf, rwbuf, s_prev, mlstm_states, prm, masks_ref):
    L = CHUNK
    pairs = range(RWKV_PAIRS)
    tri = masks_ref[MASK_LOWER]
    lower = tri > 0
    strict = masks_ref[MASK_STRICT] > 0
    same_blk = {s: masks_ref[MASK_SAME[s]] for s in MASK_SAME}
    off_blk = {s: masks_ref[MASK_OFF[s]] for s in MASK_OFF}

    z_now = rwbuf[8:8 + L, :]
    z_prev = rwbuf[7:7 + L, :]
    zm = z_now + (z_prev - z_now) * prm["mu"][...]
    yield 2
    r = zm[:, 0:D_RWKV]
    k = zm[:, D_RWKV:2 * D_RWKV]
    v = zm[:, 2 * D_RWKV:3 * D_RWKV]
    xwa = zm[:, 3 * D_RWKV:3 * D_RWKV + 128]
    xg = zm[:, 3 * D_RWKV + 128:3 * D_RWKV + 256]
    lane128 = lax.broadcasted_iota(jnp.int32, (L, 128), 1)
    lora_in = jnp.where(lane128 < DECAY_LORA, jnp.tanh(xwa), xwa)
    yield 1
    lora = _bdot(lora_in, prm["wa"][...])
    w_raw = -_softplus(-(prm["w0"][...] + lora[:, :D_RWKV])) - 0.5
    logw = -jnp.exp(w_raw)
    yield 1
    iclr = _sigmoid(prm["a0"][...] + lora[:, D_RWKV:])
    gate = _bdot(_sigmoid(xg), prm["gup"][...])
    yield 1

    seg = prm["seg"][...]
    kk = k * prm["kk"][...]
    k2 = k * (1.0 + (iclr - 1.0) * prm["ka"][...])
    yield 1
    seg_sums = _head_sums(jnp.concatenate([kk * kk, r * k2 * prm["rk"][...]], axis=0), seg)
    yield 1
    kk = kk / jnp.maximum(jnp.sqrt(seg_sums[:L]), 1e-12)
    bonus = seg_sums[L:]
    a_vec = -kk
    b_vec = kk * iclr
    yield 1

    cum = _exact_left_dot(tri, logw)
    c_ref = cum[L // 2 - 1:L // 2, :]
    c_end = cum[L - 1:L, :]
    yield 1
    e_pos = jnp.exp(cum - c_ref)
    e_neg = jnp.exp(c_ref - cum)
    yield 1
    r_rel = r * e_pos
    a_rel = a_vec * jnp.exp(-logw) * e_pos
    k_rel = k2 * e_neg
    b_rel = b_vec * e_neg
    e_ref = jnp.exp(c_ref)
    e_tail = jnp.exp(c_end - c_ref)
    e_end = jnp.exp(c_end)
    yield 1

    lane_lo = lane128 < RWKV_HEAD_DIM
    lane_lo2 = lax.broadcasted_iota(jnp.int32, (2 * L, 128), 1) < RWKV_HEAD_DIM
    blockdiag = same_blk[RWKV_HEAD_DIM].astype(F32)

    pair_slices = [slice(p * 128, (p + 1) * 128) for p in pairs]
    m_ab, m_ak, m_r = [], [], []
    ar_rel, bk_rel = [], []
    zero = jnp.zeros((), BF16)
    for ps in pair_slices:
        ar_rel.append(jnp.concatenate([a_rel[:, ps], r_rel[:, ps]], axis=0).astype(BF16))
        bk_rel.append(jnp.concatenate([b_rel[:, ps], k_rel[:, ps]], axis=0).astype(BF16))
        ar_split = jnp.concatenate([jnp.where(lane_lo2, ar_rel[-1], zero),
                                    jnp.where(lane_lo2, zero, ar_rel[-1])], axis=0)
        g_both = _bdot_nt(ar_split, bk_rel[-1]).astype(BF16)
        for half in range(2):
            g_mat = g_both[2 * L * half:2 * L * (half + 1)]
            m_ab.append(jnp.where(strict, g_mat[:L, :L], zero))
            m_ak.append(jnp.where(strict, g_mat[:L, L:], zero))
            m_r.append(jnp.concatenate([jnp.where(lower, g_mat[L:, :L], zero),
                                        jnp.where(lower, g_mat[L:, L:], zero)], axis=1))
        yield
    t_inv, (m_outs, m_new) = yield from _interleaved(
        _unit_lower_inverses(m_ab, same_blk, off_blk),
        _mlstm_chunk(zvo, qkbuf, prm["conv"], prm["gbias"], prm["mnorm"], mlstm_states, tri, lower))

    def both_heads(stacked):
        return jnp.where(lane_lo, stacked[:L], stacked[L:])

    def two_pairs(lhs0, lhs1, rhs0, rhs1, rhs_transposed=False):
        z = jnp.zeros((128, 128), BF16)
        rhs = jnp.concatenate([jnp.concatenate([rhs0.astype(BF16), z], axis=1),
                               jnp.concatenate([z, rhs1.astype(BF16)], axis=1)], axis=0)
        lhs = jnp.concatenate([lhs0.astype(BF16), lhs1.astype(BF16)], axis=1)
        out = _bdot_nt(lhs, rhs) if rhs_transposed else jnp.dot(lhs, rhs, preferred_element_type=F32)
        return out[:, :128], out[:, 128:]

    v_ps = [v[:, ps].astype(BF16) for ps in pair_slices]
    s_at_ref = [s_prev[p] * e_ref[:, ps] for p, ps in enumerate(pair_slices)]
    inter, rhs = [None] * RWKV_PAIRS, [None] * RWKV_PAIRS
    for p in range(0, RWKV_PAIRS, 2):
        inter[p], inter[p + 1] = two_pairs(ar_rel[p], ar_rel[p + 1], s_at_ref[p], s_at_ref[p + 1],
                                           rhs_transposed=True)
        mv0, mv1 = two_pairs(jnp.concatenate([m_ak[2 * p], m_ak[2 * p + 1]], axis=0),
                             jnp.concatenate([m_ak[2 * p + 2], m_ak[2 * p + 3]], axis=0), v_ps[p], v_ps[p + 1])
        rhs[p] = inter[p][:L] + both_heads(mv0)
        rhs[p + 1] = inter[p + 1][:L] + both_heads(mv1)
    yield
    u_ps = [None] * RWKV_PAIRS
    for p in range(0, RWKV_PAIRS, 2):
        t0, t1 = t_inv[p], t_inv[p + 1]
        u0, u1 = two_pairs(jnp.concatenate([t0[:, :L], t0[:, L:]], axis=0),
                           jnp.concatenate([t1[:, :L], t1[:, L:]], axis=0), rhs[p], rhs[p + 1])
        u_ps[p], u_ps[p + 1] = both_heads(u0), both_heads(u1)
    uv = [jnp.concatenate([u_ps[p].astype(BF16), v_ps[p]], axis=0) for p in pairs]
    yield
    y_parts = [inter[p][L:] + both_heads(_bdot(jnp.concatenate([m_r[2 * p], m_r[2 * p + 1]], axis=0), uv[p]))
               for p in pairs]
    s_new = [s_prev[p] * e_end[:, ps] + (blockdiag * e_tail[:, ps]) * _bdot_tn(uv[p], bk_rel[p])
             for p, ps in enumerate(pair_slices)]
    yield

    y = jnp.concatenate(y_parts, axis=1)
    inv_n = 1.0 / RWKV_HEAD_DIM
    mean = _head_sums(y, seg) * inv_n
    yield 1
    yc = y - mean
    var = _head_sums(yc * yc, seg) * inv_n
    yield 1
    y = yc * lax.rsqrt(var + RWKV_LN_EPS) * prm["lnw"][...] + prm["lnb"][...]
    y = (y + bonus * v) * gate
    return m_outs, m_new, y, s_new


MIXER_PARAMS = ("conv", "gbias", "mnorm", "mu", "w0", "a0", "wa", "gup", "kk", "ka", "rk", "lnw", "lnb", "seg")


def _mixer_kernel(xn_ref, x0_ref, gin_ref, win_ref, *rest):
    prm = dict(zip(MIXER_PARAMS, rest[:len(MIXER_PARAMS)]))
    masks_ref, y_ref, znext, zvo, qkbuf, rwbuf, ct_ref, n_ref, m_ref, s_ref = rest[len(MIXER_PARAMS):]
    L = CHUNK
    t_idx = pl.program_id(1)

    @pl.when((t_idx == 0) & (pl.program_id(0) == 0))
    def _():
        u0 = _rms(x0_ref[...].reshape(STREAMS * L, D_MODEL), gin_ref[...]).astype(BF16)
        znext[...] = jnp.dot(u0, win_ref[...], preferred_element_type=F32)

    @pl.when(t_idx == 0)
    def _():
        qkbuf[...] = jnp.zeros_like(qkbuf)
        rwbuf[...] = jnp.zeros_like(rwbuf)
        ct_ref[...] = jnp.zeros_like(ct_ref)
        n_ref[...] = jnp.zeros_like(n_ref)
        m_ref[...] = jnp.zeros_like(m_ref)
        s_ref[...] = jnp.zeros_like(s_ref)

    @pl.when(t_idx > 0)
    def _():
        for q in range(STREAMS):
            qkbuf[q, 0:8, :] = qkbuf[q, L:L + 8, :]
            rwbuf[q, 0:8, :] = rwbuf[q, L:L + 8, :]

    for q in range(STREAMS):
        rows = slice(q * L, (q + 1) * L)
        qkbuf[q, 8:8 + L, :] = znext[rows, C_QK:C_QK + 2 * D_MLSTM]
        zvo[q] = znext[rows, C_MV:C_RW]
        rwbuf[q, 8:8 + L, :] = znext[rows, C_RW:C_END]

    streams = []
    for q in range(STREAMS):
        s_prev = [s_ref[q, p] for p in range(RWKV_PAIRS)]
        mlstm_states = [(ct_ref[q, h], n_ref[q, h], m_ref[q, h]) for h in range(MLSTM_HEADS)]
        streams.append(_chunk_stream(zvo.at[q], qkbuf.at[q], rwbuf.at[q], s_prev, mlstm_states, prm, masks_ref))

    results = _run(_interleaved(*streams, filler=_project_next(xn_ref, gin_ref, win_ref, znext)))

    for q, (m_outs, m_new, y, s_new) in enumerate(results):
        for h in range(MLSTM_HEADS):
            y_ref[q, :, h * MLSTM_HEAD_DIM:(h + 1) * MLSTM_HEAD_DIM] = m_outs[h].astype(y_ref.dtype)
            ct_ref[q, h], n_ref[q, h], m_ref[q, h] = m_new[h]
        y_ref[q, :, D_MLSTM:] = y.astype(y_ref.dtype)
        for p in range(RWKV_PAIRS):
            s_ref[q, p] = s_new[p]


def _mixer(x, gin, w_in, mixer_params, masks):
    b, t, d = x.shape
    nt = t // CHUNK
    const = lambda a: pl.BlockSpec(a.shape, lambda i, j: (0,) * a.ndim, pipeline_mode=pl.Buffered(1))
    params = (gin, w_in) + tuple(mixer_params) + (masks,)
    blk = (STREAMS, CHUNK, d)
    groups = b // STREAMS

    def next_chunk(i, j):
        wrap = j + 1 == nt
        return (jnp.where(wrap, jnp.minimum(i + 1, groups - 1), i), jnp.where(wrap, 0, j + 1), 0)

    return pl.pallas_call(
        _mixer_kernel,
        grid=(b // STREAMS, nt),
        in_specs=[pl.BlockSpec(blk, next_chunk),
                  pl.BlockSpec(blk, lambda i, j: (0, 0, 0), pipeline_mode=pl.Buffered(1))]
                 + [const(a) for a in params],
        out_specs=pl.BlockSpec(blk, lambda i, j: (i, j, 0)),
        out_shape=jax.ShapeDtypeStruct((b, t, d), BF16),
        scratch_shapes=[
            pltpu.VMEM((STREAMS * CHUNK, C_END), F32),
            pltpu.VMEM((STREAMS, CHUNK, C_RW - C_MV), F32),
            pltpu.VMEM((STREAMS, CHUNK + 8, 2 * D_MLSTM), F32),
            pltpu.VMEM((STREAMS, CHUNK + 8, D_RW_IN), F32),
            pltpu.VMEM((STREAMS, MLSTM_HEADS, MLSTM_HEAD_DIM, MLSTM_HEAD_DIM), F32),
            pltpu.VMEM((STREAMS, MLSTM_HEADS, 1, MLSTM_HEAD_DIM), F32),
            pltpu.VMEM((STREAMS, MLSTM_HEADS, 1, 128), F32),
            pltpu.VMEM((STREAMS, RWKV_PAIRS, 128, 128), F32),
        ],
        compiler_params=pltpu.CompilerParams(dimension_semantics=("arbitrary", "arbitrary"),
                                             vmem_limit_bytes=VMEM_LIMIT),
        name="mixer",
    )(x, x, *params)


def _tail_kernel(x_ref, y_ref, kv_ref, wmix_ref, gx_ref, wq_ref, wo_ref, gf_ref, wg_ref, wu_ref, wd_ref,
                 gfin_ref, o_ref):
    h = x_ref[...] + jnp.dot(y_ref[...], wmix_ref[...], preferred_element_type=F32)

    q = _bdot(_rms(h, gx_ref[...]), wq_ref[...]).astype(BF16)
    heads = []
    for i in range(XATTN_HEADS):
        hs = slice(i * XATTN_HEAD_DIM, (i + 1) * XATTN_HEAD_DIM)
        k_h = kv_ref[0, :, i * XATTN_HEAD_DIM:(i + 1) * XATTN_HEAD_DIM]
        v_h = kv_ref[0, :, D_MODEL + i * XATTN_HEAD_DIM:D_MODEL + (i + 1) * XATTN_HEAD_DIM]
        s = _bdot_nt(q[:, hs], k_h) * (XATTN_HEAD_DIM ** -0.5)
        e = jnp.exp(s - jnp.max(s, axis=-1, keepdims=True))
        heads.append((_bdot(e, v_h) / jnp.sum(e, axis=-1, keepdims=True)).astype(BF16))
    h = h + jnp.dot(jnp.concatenate(heads, axis=1), wo_ref[...], preferred_element_type=F32)

    u = _rms(h, gf_ref[...]).astype(BF16)
    for c0 in range(0, wg_ref.shape[1], FF_COLS):
        gate = jnp.dot(u, wg_ref[:, c0:c0 + FF_COLS], preferred_element_type=F32)
        up = jnp.dot(u, wu_ref[:, c0:c0 + FF_COLS], preferred_element_type=F32)
        act = (gate * _sigmoid(gate) * up).astype(BF16)
        h = h + jnp.dot(act, wd_ref[c0:c0 + FF_COLS, :], preferred_element_type=F32)

    o_ref[...] = _rms(h, gfin_ref[...])


def _tail(x2, y2, kv, tail_params, tm, t):
    n, d = x2.shape
    per_b = t // tm
    const = lambda a: pl.BlockSpec(a.shape, lambda i: (0,) * a.ndim, pipeline_mode=pl.Buffered(1))
    return pl.pallas_call(
        _tail_kernel,
        grid=(n // tm,),
        in_specs=[pl.BlockSpec((tm, d), lambda i: (i, 0)),
                  pl.BlockSpec((tm, d), lambda i: (i, 0)),
                  pl.BlockSpec((1,) + kv.shape[1:], lambda i: (i // per_b, 0, 0))]
                 + [const(a) for a in tail_params],
        out_specs=pl.BlockSpec((tm, d), lambda i: (i, 0)),
        out_shape=jax.ShapeDtypeStruct((n, d), F32),
        compiler_params=pltpu.CompilerParams(dimension_semantics=("parallel",),
                                             vmem_limit_bytes=VMEM_LIMIT),
        name="tail",
    )(x2, y2, kv, *tail_params)


def _chunk_masks():
    i = jnp.arange(CHUNK)
    row, col = i[:, None], i[None, :]
    planes = [None] * 8
    planes[MASK_LOWER] = row >= col
    planes[MASK_STRICT] = row > col
    for s, idx in MASK_SAME.items():
        planes[idx] = (row // s) == (col // s)
    for s, idx in MASK_OFF.items():
        planes[idx] = ((row // (2 * s)) == (col // (2 * s))) & ((row // s) > (col // s))
    return jnp.stack(planes).astype(BF16)


def _regroup_w_in(w):
    n_m = 4 * D_MLSTM
    gates = w[:, n_m:n_m + 2 * MLSTM_HEADS]
    gates = jnp.pad(gates, ((0, 0), (0, 128 - 2 * MLSTM_HEADS)))
    return jnp.concatenate([w[:, :n_m], gates, w[:, n_m + 2 * MLSTM_HEADS:]], axis=1).astype(BF16)


def kernel(x, mem, norm_mix, w_in, mlstm_conv, mlstm_i_bias, mlstm_f_bias, mlstm_norm, rwkv_mu, rwkv_w0, rwkv_w_up, rwkv_a0, rwkv_a_up, rwkv_g_up, rwkv_k_k, rwkv_k_a, rwkv_r_k, rwkv_ln_w, rwkv_ln_b, w_mix_out, norm_xattn, norm_mem, xattn_wq, xattn_wkv, xattn_wo, norm_ffn, ffn_w_gate, ffn_w_up, ffn_w_down, norm_final):
    b, t, d = x.shape
    assert d == D_MODEL and t % CHUNK == 0 and b % STREAMS == 0 and norm_mix.shape[0] == 1
    row = lambda a: a.reshape(1, -1).astype(F32)

    kv = _mem_kv(mem, row(norm_mem[0]), xattn_wkv[0].astype(BF16))

    gbias = jnp.pad(jnp.concatenate([mlstm_i_bias[0], mlstm_f_bias[0]]), (0, 128 - 2 * MLSTM_HEADS))
    zeros = jnp.zeros((DECAY_LORA, D_RWKV), F32)
    wa = jnp.concatenate([jnp.concatenate([rwkv_w_up[0], zeros], axis=1),
                          jnp.concatenate([zeros, rwkv_a_up[0]], axis=1)], axis=0).astype(BF16)
    head_id = jnp.arange(D_RWKV // 2) // RWKV_HEAD_DIM
    seg = (head_id[:, None] == head_id[None, :]).astype(BF16)
    mixer_params = (mlstm_conv[0], row(gbias), row(mlstm_norm[0]), row(rwkv_mu[0]), row(rwkv_w0[0]),
                    row(rwkv_a0[0]), wa, rwkv_g_up[0].astype(BF16), row(rwkv_k_k[0]), row(rwkv_k_a[0]),
                    row(rwkv_r_k[0]), row(rwkv_ln_w[0]), row(rwkv_ln_b[0]), seg)
    tail_params = (w_mix_out[0].astype(BF16), row(norm_xattn[0]), xattn_wq[0].astype(BF16),
                   xattn_wo[0].astype(BF16), row(norm_ffn[0]), ffn_w_gate[0].astype(BF16),
                   ffn_w_up[0].astype(BF16), ffn_w_down[0].astype(BF16), row(norm_final))
    y = _mixer(x, row(norm_mix[0]), _regroup_w_in(w_in[0]), mixer_params, _chunk_masks())
    tm = TAIL_ROWS if t % TAIL_ROWS == 0 else CHUNK
    out = _tail(x.reshape(b * t, d), y.reshape(b * t, d), kv, tail_params, tm, t)
    return out.reshape(b, t, d)
```

```python
import functools
import math

import jax
import jax.numpy as jnp
from jax import lax
from jax.experimental import pallas as pl
from jax.experimental.pallas import tpu as pltpu

F32 = jnp.float32
BF16 = jnp.bfloat16

EPS = 1e-6
RWKV_LN_EPS = 64e-5
CONV_WIDTH = 4

D_MODEL = 1024
D_MLSTM = 512
D_RWKV = 512
MLSTM_HEADS = 4
MLSTM_HEAD_DIM = 128
RWKV_HEAD_DIM = 64
RWKV_PAIRS = D_RWKV // (2 * RWKV_HEAD_DIM)
DECAY_LORA = 64
ICLR_LORA = 64
GATE_LORA = 128
XATTN_HEADS = 4
XATTN_HEAD_DIM = 256
CHUNK = 128
INV_BASE = 16
PROJ_COLS = 256
STREAMS = 2
FF_COLS = 256
TAIL_ROWS = 512

MASK_LOWER, MASK_STRICT = 0, 1
MASK_SAME = {16: 2, 32: 3, 64: 4}
MASK_OFF = {16: 5, 32: 6, 64: 7}

C_QK = 0
C_MV = C_QK + 2 * D_MLSTM
C_MO = C_MV + D_MLSTM
C_MG = C_MO + D_MLSTM
C_RW = C_MG + 128
D_RW_IN = 3 * D_RWKV + DECAY_LORA + ICLR_LORA + GATE_LORA
C_END = C_RW + D_RW_IN

VMEM_LIMIT = 56 * 1024 * 1024


def _bdot(a, b):
    return jnp.dot(a.astype(BF16), b.astype(BF16), preferred_element_type=F32)


def _bdot_nt(a, b):
    return lax.dot_general(a.astype(BF16), b.astype(BF16), (((1,), (1,)), ((), ())),
                           preferred_element_type=F32)


def _bdot_tn(a, b):
    return lax.dot_general(a.astype(BF16), b.astype(BF16), (((0,), (0,)), ((), ())),
                           preferred_element_type=F32)


def _exact_left_dot(ones_mat, x, passes=3):
    acc, rest = None, x
    for i in range(passes):
        piece = rest.astype(BF16)
        part = jnp.dot(ones_mat, piece, preferred_element_type=F32)
        acc = part if acc is None else acc + part
        if i + 1 < passes:
            rest = rest - piece.astype(F32)
    return acc


def _head_sums(x, seg):
    n, w = x.shape
    hi = x.astype(BF16)
    lo = (x - hi.astype(F32)).astype(BF16)
    stacked = jnp.concatenate([hi[:, :w // 2], hi[:, w // 2:], lo[:, :w // 2], lo[:, w // 2:]], axis=0)
    sums = jnp.dot(stacked, seg, preferred_element_type=F32)
    return jnp.concatenate([sums[0:n] + sums[2 * n:3 * n], sums[n:2 * n] + sums[3 * n:4 * n]], axis=1)


def _rms(x, g):
    return x * lax.rsqrt(jnp.mean(x * x, axis=-1, keepdims=True) + EPS) * g


def _sigmoid(x):
    return 1.0 / (1.0 + jnp.exp(-x))


def _log_sigmoid(x):
    return jnp.minimum(x, 0.0) - jnp.log(1.0 + jnp.exp(-jnp.abs(x)))


def _mem_kv_kernel(m_ref, g_ref, w_ref, kv_ref):
    u = _rms(m_ref[0], g_ref[...]).astype(BF16)
    kv_ref[0] = jnp.dot(u, w_ref[...], preferred_element_type=F32).astype(BF16)


def _mem_kv(mem, g, w):
    b, m, d = mem.shape
    c = w.shape[1]
    return pl.pallas_call(
        _mem_kv_kernel,
        grid=(b,),
        in_specs=[pl.BlockSpec((1, m, d), lambda i: (i, 0, 0)),
                  pl.BlockSpec((1, d), lambda i: (0, 0)),
                  pl.BlockSpec((d, c), lambda i: (0, 0), pipeline_mode=pl.Buffered(1))],
        out_specs=pl.BlockSpec((1, m, c), lambda i: (i, 0, 0)),
        out_shape=jax.ShapeDtypeStruct((b, m, c), BF16),
        compiler_params=pltpu.CompilerParams(dimension_semantics=("parallel",),
                                             vmem_limit_bytes=VMEM_LIMIT),
        name="mem_kv",
    )(mem, g, w)


def _interleaved(*gens, filler=None):
    results = [None] * len(gens)
    live = list(enumerate(gens))
    while live:
        still, hint = [], 0
        for i, g in live:
            try:
                hint = max(hint, next(g) or 0)
                still.append((i, g))
            except StopIteration as stop:
                results[i] = stop.value
        live = still
        if filler is not None:
            for _ in range(hint):
                next(filler, None)
        yield hint
    if filler is not None:
        for _ in filler:
            pass
    return results


def _run(gen):
    try:
        while True:
            next(gen)
    except StopIteration as stop:
        return stop.value


def _fold_rows(x, s):
    acc = x[0:s]
    for j in range(1, x.shape[0] // s):
        acc = acc + x[j * s:(j + 1) * s]
    return acc


def _tile_rows(xc, n):
    return jnp.concatenate([xc] * (n // xc.shape[0]), axis=0)


def _unit_lower_inverses(ms, same_blk, off_blk):
    n = ms[0].shape[0]
    b = INV_BASE
    dot = functools.partial(jnp.dot, preferred_element_type=F32)
    lane = lax.broadcasted_iota(jnp.int32, (b, 2 * n), 1)
    eye_c = (lane % b == lax.broadcasted_iota(jnp.int32, (b, 2 * n), 0)).astype(F32)
    zeros = jnp.zeros((n, n), BF16)
    pairs = [(ms[j], ms[j + 1]) for j in range(0, len(ms), 2)]

    def block_diag2(lo, hi):
        return jnp.concatenate([jnp.concatenate([lo, zeros], axis=1),
                                jnp.concatenate([zeros, hi], axis=1)], axis=0)

    def fold2(pair, mask, s):
        return jnp.concatenate([_fold_rows(m * mask, s) for m in pair], axis=1)

    def expand2(xc, mask):
        return block_diag2(_tile_rows(xc[:, :n].astype(BF16), n) * mask,
                           _tile_rows(xc[:, n:].astype(BF16), n) * mask)

    m_diag = [(lo * same_blk[b], hi * same_blk[b]) for lo, hi in pairs]
    pcs = [jnp.concatenate([_fold_rows(lo, b), _fold_rows(hi, b)], axis=1) for lo, hi in m_diag]
    tcs = [eye_c + p.astype(F32) for p in pcs]
    pcs = [dot(p, block_diag2(lo, hi)) for p, (lo, hi) in zip(pcs, m_diag)]
    yield
    s = 2
    while 2 * s < b:
        xs = [dot(jnp.concatenate([t, p], axis=0).astype(BF16), expand2(p, same_blk[b]))
              for t, p in zip(tcs, pcs)]
        tcs = [t + x[:b] for t, x in zip(tcs, xs)]
        pcs = [x[b:] for x in xs]
        s *= 2
        yield
    tcs = [t + dot(t.astype(BF16), expand2(p, same_blk[b])) for t, p in zip(tcs, pcs)]
    yield
    s = b
    while s < n:
        ccs = [fold2(pr, off_blk[s], s) for pr in pairs]
        x1s = [dot(c, expand2(t, same_blk[s])) for c, t in zip(ccs, tcs)]
        yield
        x2s = [dot(t.astype(BF16), expand2(x, off_blk[s])) for t, x in zip(tcs, x1s)]
        even = (lax.broadcasted_iota(jnp.int32, (s, 2 * n), 1) // s) % 2 == 0
        tcs = [jnp.concatenate([jnp.where(even, t, 0.0), x + jnp.where(even, 0.0, t)], axis=0)
               for t, x in zip(tcs, x2s)]
        yield
        s *= 2
    return tcs


def _mlstm_chunk(zvo, qkbuf, conv_ref, gbias_ref, mnorm_ref, states, tri, lower):
    L = CHUNK
    conv_w = conv_ref[...]
    qk = qkbuf[8:8 + L, :] * conv_w[CONV_WIDTH - 1:CONV_WIDTH, :]
    for j in range(1, CONV_WIDTH):
        qk += qkbuf[8 - j:8 - j + L, :] * conv_w[CONV_WIDTH - 1 - j:CONV_WIDTH - j, :]
    yield 2
    qk = qk * _sigmoid(qk)
    q_all = qk[:, :D_MLSTM]
    k_all = qk[:, D_MLSTM:] * (MLSTM_HEAD_DIM ** -0.5)
    yield 2

    gates = zvo[:, C_MG - C_MV:C_MG - C_MV + 128] + gbias_ref[...]
    bcum = _exact_left_dot(tri, _log_sigmoid(gates))
    gates_t = gates.T
    bcum_t = bcum.T

    outs, new_states = [], []
    for h in range(MLSTM_HEADS):
        hs = slice(h * MLSTM_HEAD_DIM, (h + 1) * MLSTM_HEAD_DIM)
        qh = q_all[:, hs].astype(BF16)
        kh = k_all[:, hs]
        vh = zvo[:, h * MLSTM_HEAD_DIM:(h + 1) * MLSTM_HEAD_DIM]
        oh = zvo[:, C_MO - C_MV + h * MLSTM_HEAD_DIM:C_MO - C_MV + (h + 1) * MLSTM_HEAD_DIM]
        f_lane = MLSTM_HEADS + h
        b_col = bcum[:, f_lane:f_lane + 1]
        b_row = bcum_t[f_lane:f_lane + 1, :]
        ig_col = gates[:, h:h + 1]
        ig_row = gates_t[h:h + 1, :]
        g_tot = bcum[L - 1:L, f_lane:f_lane + 1]
        ct_prev, n_prev, m_prev = states[h]
        m_prev = m_prev[:, 0:1]

        dm = jnp.where(lower, b_col - b_row + ig_row, -jnp.inf)
        m_intra = jnp.max(dm, axis=-1, keepdims=True)
        m_inter = b_col + m_prev
        m_t = jnp.maximum(m_inter, m_intra)
        s_mat = _bdot_nt(qh, kh) * jnp.exp(dm - m_t)
        s_inter = jnp.exp(m_inter - m_t)
        num = _bdot(s_mat, vh) + s_inter * _bdot(qh, ct_prev)
        den = (jnp.sum(s_mat, axis=-1, keepdims=True)
               + s_inter * jnp.sum(q_all[:, hs] * n_prev, axis=-1, keepdims=True))
        hh = num / jnp.maximum(jnp.abs(den), jnp.exp(-m_t))
        yield

        a_col = g_tot - b_col + ig_col
        m_loc = jnp.max(a_col, axis=0, keepdims=True)
        wgt = jnp.exp(a_col - m_loc)
        d_ct = _bdot_tn(kh, vh * wgt)
        d_n = jnp.sum(kh * wgt, axis=0, keepdims=True)
        m_new = jnp.maximum(g_tot + m_prev, m_loc)
        s_old = jnp.exp(g_tot + m_prev - m_new)
        s_new = jnp.exp(m_loc - m_new)
        new_states.append((s_old * ct_prev + s_new * d_ct, s_old * n_prev + s_new * d_n,
                           jnp.broadcast_to(m_new, (1, 128))))

        hh = hh * lax.rsqrt(jnp.mean(hh * hh, axis=-1, keepdims=True) + EPS)
        outs.append(hh * mnorm_ref[:, hs] * _sigmoid(oh))
        yield
    return outs, new_states


def _project_next(x_ref, g_ref, w_ref, znext):
    u = _rms(x_ref[...].reshape(STREAMS * CHUNK, D_MODEL), g_ref[...]).astype(BF16)
    yield
    for c0 in range(0, C_END, PROJ_COLS):
        c1 = min(c0 + PROJ_COLS, C_END)
        znext[:, c0:c1] = jnp.dot(u, w_ref[:, c0:c1], preferred_element_type=F32)
        yield


def _chunk_stream(zvo, qkbuf, rwbuf, s_prev, mlstm_states, prm, masks_ref):
    L = CHUNK
    pairs = range(RWKV_PAIRS)
    tri = masks_ref[MASK_LOWER]
    lower = tri > 0
    strict = masks_ref[MASK_STRICT] > 0
    same_blk = {s: masks_ref[MASK_SAME[s]] for s in MASK_SAME}
    off_blk = {s: masks_ref[MASK_OFF[s]] for s in MASK_OFF}

    z_now = rwbuf[8:8 + L, :]
    z_prev = rwbuf[7:7 + L, :]
    zm = z_now + (z_prev - z_now) * prm["mu"][...]
    yield 2
    r = zm[:, 0:D_RWKV]
    k = zm[:, D_RWKV:2 * D_RWKV]
    v = zm[:, 2 * D_RWKV:3 * D_RWKV]
    xwa = zm[:, 3 * D_RWKV:3 * D_RWKV + 128]
    xg = zm[:, 3 * D_RWKV + 128:3 * D_RWKV + 256]
    lane128 = lax.broadcasted_iota(jnp.int32, (L, 128), 1)
    lora_in = jnp.where(lane128 < DECAY_LORA, jnp.tanh(xwa), xwa)
    yield 1
    lora = _bdot(lora_in, prm["wa"][...])
    logw = -math.exp(-0.5) * _sigmoid(prm["w0"][...] + lora[:, :D_RWKV])
    yield 1
    iclr = _sigmoid(prm["a0"][...] + lora[:, D_RWKV:])
    gate = _bdot(_sigmoid(xg), prm["gup"][...])
    yield 1

    seg = prm["seg"][...]
    kk = k * prm["kk"][...]
    k2 = k * (1.0 + (iclr - 1.0) * prm["ka"][...])
    yield 1
    seg_sums = _head_sums(jnp.concatenate([kk * kk, r * k2 * prm["rk"][...]], axis=0), seg)
    yield 1
    kk = kk * lax.rsqrt(jnp.maximum(seg_sums[:L], 1e-24))
    bonus = seg_sums[L:]
    a_vec = -kk
    b_vec = kk * iclr
    yield 1

    cum = _exact_left_dot(tri, logw, passes=2)
    c_ref = cum[L // 2 - 1:L // 2, :]
    c_end = cum[L - 1:L, :]
    yield 1
    d_ref = cum - c_ref
    e_pos = jnp.exp(d_ref)
    e_neg = 1.0 / e_pos
    yield 1
    r_rel = r * e_pos
    a_rel = a_vec * jnp.exp(d_ref - logw)
    k_rel = k2 * e_neg
    b_rel = b_vec * e_neg
    e_ref = jnp.exp(c_ref)
    e_tail = jnp.exp(c_end - c_ref)
    e_end = jnp.exp(c_end)
    yield 1

    lane_lo = lane128 < RWKV_HEAD_DIM
    lane_lo2 = lax.broadcasted_iota(jnp.int32, (2 * L, 128), 1) < RWKV_HEAD_DIM
    blockdiag = same_blk[RWKV_HEAD_DIM].astype(F32)

    pair_slices = [slice(p * 128, (p + 1) * 128) for p in pairs]
    m_ab, m_ak, m_r = [], [], []
    ar_rel, bk_rel = [], []
    zero = jnp.zeros((), BF16)
    for ps in pair_slices:
        ar_rel.append(jnp.concatenate([a_rel[:, ps], r_rel[:, ps]], axis=0).astype(BF16))
        bk_rel.append(jnp.concatenate([b_rel[:, ps], k_rel[:, ps]], axis=0).astype(BF16))
        ar_split = jnp.concatenate([jnp.where(lane_lo2, ar_rel[-1], zero),
                                    jnp.where(lane_lo2, zero, ar_rel[-1])], axis=0)
        g_both = _bdot_nt(ar_split, bk_rel[-1]).astype(BF16)
        for half in range(2):
            g_mat = g_both[2 * L * half:2 * L * (half + 1)]
            m_ab.append(jnp.where(strict, g_mat[:L, :L], zero))
            m_ak.append(jnp.where(strict, g_mat[:L, L:], zero))
            m_r.append(jnp.concatenate([jnp.where(lower, g_mat[L:, :L], zero),
                                        jnp.where(lower, g_mat[L:, L:], zero)], axis=1))
        yield
    t_inv, (m_outs, m_new) = yield from _interleaved(
        _unit_lower_inverses(m_ab, same_blk, off_blk),
        _mlstm_chunk(zvo, qkbuf, prm["conv"], prm["gbias"], prm["mnorm"], mlstm_states, tri, lower))

    def both_heads(stacked):
        return jnp.where(lane_lo, stacked[:L], stacked[L:])

    def two_pairs(lhs0, lhs1, rhs0, rhs1, rhs_transposed=False):
        z = jnp.zeros((128, 128), BF16)
        rhs = jnp.concatenate([jnp.concatenate([rhs0.astype(BF16), z], axis=1),
                               jnp.concatenate([z, rhs1.astype(BF16)], axis=1)], axis=0)
        lhs = jnp.concatenate([lhs0.astype(BF16), lhs1.astype(BF16)], axis=1)
        out = _bdot_nt(lhs, rhs) if rhs_transposed else jnp.dot(lhs, rhs, preferred_element_type=F32)
        return out[:, :128], out[:, 128:]

    v_ps = [v[:, ps].astype(BF16) for ps in pair_slices]
    s_at_ref = [s_prev[p] * e_ref[:, ps] for p, ps in enumerate(pair_slices)]
    inter, rhs = [None] * RWKV_PAIRS, [None] * RWKV_PAIRS
    for p in range(0, RWKV_PAIRS, 2):
        inter[p], inter[p + 1] = two_pairs(ar_rel[p], ar_rel[p + 1], s_at_ref[p], s_at_ref[p + 1],
                                           rhs_transposed=True)
        mv0, mv1 = two_pairs(jnp.concatenate([m_ak[2 * p], m_ak[2 * p + 1]], axis=0),
                             jnp.concatenate([m_ak[2 * p + 2], m_ak[2 * p + 3]], axis=0), v_ps[p], v_ps[p + 1])
        rhs[p] = inter[p][:L] + both_heads(mv0)
        rhs[p + 1] = inter[p + 1][:L] + both_heads(mv1)
    yield
    u_ps = [None] * RWKV_PAIRS
    for p in range(0, RWKV_PAIRS, 2):
        t0, t1 = t_inv[p], t_inv[p + 1]
        u0, u1 = two_pairs(jnp.concatenate([t0[:, :L], t0[:, L:]], axis=0),
                           jnp.concatenate([t1[:, :L], t1[:, L:]], axis=0), rhs[p], rhs[p + 1])
        u_ps[p], u_ps[p + 1] = both_heads(u0), both_heads(u1)
    uv = [jnp.concatenate([u_ps[p].astype(BF16), v_ps[p]], axis=0) for p in pairs]
    yield
    y_parts = [inter[p][L:] + both_heads(_bdot(jnp.concatenate([m_r[2 * p], m_r[2 * p + 1]], axis=0), uv[p]))
               for p in pairs]
    s_new = [s_prev[p] * e_end[:, ps] + (blockdiag * e_tail[:, ps]) * _bdot_tn(uv[p], bk_rel[p])
             for p, ps in enumerate(pair_slices)]
    yield

    y = jnp.concatenate(y_parts, axis=1)
    inv_n = 1.0 / RWKV_HEAD_DIM
    mean = _head_sums(y, seg) * inv_n
    yield 1
    yc = y - mean
    var = _head_sums(yc * yc, seg) * inv_n
    yield 1
    y = yc * lax.rsqrt(var + RWKV_LN_EPS) * prm["lnw"][...] + prm["lnb"][...]
    y = (y + bonus * v) * gate
    return m_outs, m_new, y, s_new


MIXER_PARAMS = ("conv", "gbias", "mnorm", "mu", "w0", "a0", "wa", "gup", "kk", "ka", "rk", "lnw", "lnb", "seg")


def _mixer_kernel(xn_ref, x0_ref, gin_ref, win_ref, *rest):
    prm = dict(zip(MIXER_PARAMS, rest[:len(MIXER_PARAMS)]))
    masks_ref, y_ref, znext, zvo, qkbuf, rwbuf, ct_ref, n_ref, m_ref, s_ref = rest[len(MIXER_PARAMS):]
    L = CHUNK
    t_idx = pl.program_id(1)

    @pl.when((t_idx == 0) & (pl.program_id(0) == 0))
    def _():
        u0 = _rms(x0_ref[...].reshape(STREAMS * L, D_MODEL), gin_ref[...]).astype(BF16)
        znext[...] = jnp.dot(u0, win_ref[...], preferred_element_type=F32)

    @pl.when(t_idx == 0)
    def _():
        qkbuf[...] = jnp.zeros_like(qkbuf)
        rwbuf[...] = jnp.zeros_like(rwbuf)
        ct_ref[...] = jnp.zeros_like(ct_ref)
        n_ref[...] = jnp.zeros_like(n_ref)
        m_ref[...] = jnp.zeros_like(m_ref)
        s_ref[...] = jnp.zeros_like(s_ref)

    @pl.when(t_idx > 0)
    def _():
        for q in range(STREAMS):
            qkbuf[q, 0:8, :] = qkbuf[q, L:L + 8, :]
            rwbuf[q, 0:8, :] = rwbuf[q, L:L + 8, :]

    for q in range(STREAMS):
        rows = slice(q * L, (q + 1) * L)
        qkbuf[q, 8:8 + L, :] = znext[rows, C_QK:C_QK + 2 * D_MLSTM]
        zvo[q] = znext[rows, C_MV:C_RW]
        rwbuf[q, 8:8 + L, :] = znext[rows, C_RW:C_END]

    streams = []
    for q in range(STREAMS):
        s_prev = [s_ref[q, p] for p in range(RWKV_PAIRS)]
        mlstm_states = [(ct_ref[q, h], n_ref[q, h], m_ref[q, h]) for h in range(MLSTM_HEADS)]
        streams.append(_chunk_stream(zvo.at[q], qkbuf.at[q], rwbuf.at[q], s_prev, mlstm_states, prm, masks_ref))

    results = _run(_interleaved(*streams, filler=_project_next(xn_ref, gin_ref, win_ref, znext)))

    for q, (m_outs, m_new, y, s_new) in enumerate(results):
        for h in range(MLSTM_HEADS):
            y_ref[q, :, h * MLSTM_HEAD_DIM:(h + 1) * MLSTM_HEAD_DIM] = m_outs[h].astype(y_ref.dtype)
            ct_ref[q, h], n_ref[q, h], m_ref[q, h] = m_new[h]
        y_ref[q, :, D_MLSTM:] = y.astype(y_ref.dtype)
        for p in range(RWKV_PAIRS):
            s_ref[q, p] = s_new[p]


def _mixer(x, gin, w_in, mixer_params, masks):
    b, t, d = x.shape
    nt = t // CHUNK
    const = lambda a: pl.BlockSpec(a.shape, lambda i, j: (0,) * a.ndim, pipeline_mode=pl.Buffered(1))
    params = (gin, w_in) + tuple(mixer_params) + (masks,)
    blk = (STREAMS, CHUNK, d)
    groups = b // STREAMS

    def next_chunk(i, j):
        wrap = j + 1 == nt
        return (jnp.where(wrap, jnp.minimum(i + 1, groups - 1), i), jnp.where(wrap, 0, j + 1), 0)

    return pl.pallas_call(
        _mixer_kernel,
        grid=(b // STREAMS, nt),
        in_specs=[pl.BlockSpec(blk, next_chunk),
                  pl.BlockSpec(blk, lambda i, j: (0, 0, 0), pipeline_mode=pl.Buffered(1))]
                 + [const(a) for a in params],
        out_specs=pl.BlockSpec(blk, lambda i, j: (i, j, 0)),
        out_shape=jax.ShapeDtypeStruct((b, t, d), BF16),
        scratch_shapes=[
            pltpu.VMEM((STREAMS * CHUNK, C_END), F32),
            pltpu.VMEM((STREAMS, CHUNK, C_RW - C_MV), F32),
            pltpu.VMEM((STREAMS, CHUNK + 8, 2 * D_MLSTM), F32),
            pltpu.VMEM((STREAMS, CHUNK + 8, D_RW_IN), F32),
            pltpu.VMEM((STREAMS, MLSTM_HEADS, MLSTM_HEAD_DIM, MLSTM_HEAD_DIM), F32),
            pltpu.VMEM((STREAMS, MLSTM_HEADS, 1, MLSTM_HEAD_DIM), F32),
            pltpu.VMEM((STREAMS, MLSTM_HEADS, 1, 128), F32),
            pltpu.VMEM((STREAMS, RWKV_PAIRS, 128, 128), F32),
        ],
        compiler_params=pltpu.CompilerParams(dimension_semantics=("arbitrary", "arbitrary"),
                                             vmem_limit_bytes=VMEM_LIMIT),
        name="mixer",
    )(x, x, *params)


def _tail_kernel(x_ref, y_ref, kv_ref, wmix_ref, gx_ref, wq_ref, wo_ref, gf_ref, wg_ref, wu_ref, wd_ref,
                 gfin_ref, o_ref):
    h = x_ref[...] + jnp.dot(y_ref[...], wmix_ref[...], preferred_element_type=F32)

    q = _bdot(_rms(h, gx_ref[...]), wq_ref[...]).astype(BF16)
    heads = []
    for i in range(XATTN_HEADS):
        hs = slice(i * XATTN_HEAD_DIM, (i + 1) * XATTN_HEAD_DIM)
        k_h = kv_ref[0, :, i * XATTN_HEAD_DIM:(i + 1) * XATTN_HEAD_DIM]
        v_h = kv_ref[0, :, D_MODEL + i * XATTN_HEAD_DIM:D_MODEL + (i + 1) * XATTN_HEAD_DIM]
        s = _bdot_nt(q[:, hs], k_h) * (XATTN_HEAD_DIM ** -0.5)
        e = jnp.exp(s - jnp.max(s, axis=-1, keepdims=True))
        heads.append((_bdot(e, v_h) / jnp.sum(e, axis=-1, keepdims=True)).astype(BF16))
    h = h + jnp.dot(jnp.concatenate(heads, axis=1), wo_ref[...], preferred_element_type=F32)

    u = _rms(h, gf_ref[...]).astype(BF16)
    for c0 in range(0, wg_ref.shape[1], FF_COLS):
        gate = jnp.dot(u, wg_ref[:, c0:c0 + FF_COLS], preferred_element_type=F32)
        up = jnp.dot(u, wu_ref[:, c0:c0 + FF_COLS], preferred_element_type=F32)
        act = (gate * _sigmoid(gate) * up).astype(BF16)
        h = h + jnp.dot(act, wd_ref[c0:c0 + FF_COLS, :], preferred_element_type=F32)

    o_ref[...] = _rms(h, gfin_ref[...])


def _tail(x2, y2, kv, tail_params, tm, t):
    n, d = x2.shape
    per_b = t // tm
    const = lambda a: pl.BlockSpec(a.shape, lambda i: (0,) * a.ndim, pipeline_mode=pl.Buffered(1))
    return pl.pallas_call(
        _tail_kernel,
        grid=(n // tm,),
        in_specs=[pl.BlockSpec((tm, d), lambda i: (i, 0)),
                  pl.BlockSpec((tm, d), lambda i: (i, 0)),
                  pl.BlockSpec((1,) + kv.shape[1:], lambda i: (i // per_b, 0, 0))]
                 + [const(a) for a in tail_params],
        out_specs=pl.BlockSpec((tm, d), lambda i: (i, 0)),
        out_shape=jax.ShapeDtypeStruct((n, d), F32),
        compiler_params=pltpu.CompilerParams(dimension_semantics=("parallel",),
                                             vmem_limit_bytes=VMEM_LIMIT),
        name="tail",
    )(x2, y2, kv, *tail_params)


def _chunk_masks():
    i = jnp.arange(CHUNK)
    row, col = i[:, None], i[None, :]
    planes = [None] * 8
    planes[MASK_LOWER] = row >= col
    planes[MASK_STRICT] = row > col
    for s, idx in MASK_SAME.items():
        planes[idx] = (row // s) == (col // s)
    for s, idx in MASK_OFF.items():
        planes[idx] = ((row // (2 * s)) == (col // (2 * s))) & ((row // s) > (col // s))
    return jnp.stack(planes).astype(BF16)


def _regroup_w_in(w):
    n_m = 4 * D_MLSTM
    gates = w[:, n_m:n_m + 2 * MLSTM_HEADS]
    gates = jnp.pad(gates, ((0, 0), (0, 128 - 2 * MLSTM_HEADS)))
    return jnp.concatenate([w[:, :n_m], gates, w[:, n_m + 2 * MLSTM_HEADS:]], axis=1).astype(BF16)


def kernel(x, mem, norm_mix, w_in, mlstm_conv, mlstm_i_bias, mlstm_f_bias, mlstm_norm, rwkv_mu, rwkv_w0, rwkv_w_up, rwkv_a0, rwkv_a_up, rwkv_g_up, rwkv_k_k, rwkv_k_a, rwkv_r_k, rwkv_ln_w, rwkv_ln_b, w_mix_out, norm_xattn, norm_mem, xattn_wq, xattn_wkv, xattn_wo, norm_ffn, ffn_w_gate, ffn_w_up, ffn_w_down, norm_final):
    b, t, d = x.shape
    assert d == D_MODEL and t % CHUNK == 0 and b % STREAMS == 0 and norm_mix.shape[0] == 1
    row = lambda a: a.reshape(1, -1).astype(F32)

    kv = _mem_kv(mem, row(norm_mem[0]), xattn_wkv[0].astype(BF16))

    gbias = jnp.pad(jnp.concatenate([mlstm_i_bias[0], mlstm_f_bias[0]]), (0, 128 - 2 * MLSTM_HEADS))
    zeros = jnp.zeros((DECAY_LORA, D_RWKV), F32)
    wa = jnp.concatenate([jnp.concatenate([rwkv_w_up[0], zeros], axis=1),
                          jnp.concatenate([zeros, rwkv_a_up[0]], axis=1)], axis=0).astype(BF16)
    head_id = jnp.arange(D_RWKV // 2) // RWKV_HEAD_DIM
    seg = (head_id[:, None] == head_id[None, :]).astype(BF16)
    mixer_params = (mlstm_conv[0], row(gbias), row(mlstm_norm[0]), row(rwkv_mu[0]), row(rwkv_w0[0]),
                    row(rwkv_a0[0]), wa, rwkv_g_up[0].astype(BF16), row(rwkv_k_k[0]), row(rwkv_k_a[0]),
                    row(rwkv_r_k[0]), row(rwkv_ln_w[0]), row(rwkv_ln_b[0]), seg)
    tail_params = (w_mix_out[0].astype(BF16), row(norm_xattn[0]), xattn_wq[0].astype(BF16),
                   xattn_wo[0].astype(BF16), row(norm_ffn[0]), ffn_w_gate[0].astype(BF16),
                   ffn_w_up[0].astype(BF16), ffn_w_down[0].astype(BF16), row(norm_final))
    y = _mixer(x, row(norm_mix[0]), _regroup_w_in(w_in[0]), mixer_params, _chunk_masks())
    tm = TAIL_ROWS if t % TAIL_ROWS == 0 else CHUNK
    out = _tail(x.reshape(b * t, d), y.reshape(b * t, d), kv, tail_params, tm, t)
    return out.reshape(b, t, d)
```

```python
import functools
import math

import jax
import jax.numpy as jnp
from jax import lax
from jax.experimental import pallas as pl
from jax.experimental.pallas import tpu as pltpu

F32 = jnp.float32
BF16 = jnp.bfloat16

EPS = 1e-6
RWKV_LN_EPS = 64e-5
CONV_WIDTH = 4

D_MODEL = 1024
D_MLSTM = 512
D_RWKV = 512
MLSTM_HEADS = 4
MLSTM_HEAD_DIM = 128
RWKV_HEAD_DIM = 64
RWKV_PAIRS = D_RWKV // (2 * RWKV_HEAD_DIM)
DECAY_LORA = 64
ICLR_LORA = 64
GATE_LORA = 128
XATTN_HEADS = 4
XATTN_HEAD_DIM = 256
CHUNK = 128
INV_BASE = 16
PROJ_COLS = 256
MLSTM_EARLY_ROUNDS = 2
STREAMS = 2
FF_COLS = 256
TAIL_ROWS = 512

MASK_LOWER, MASK_STRICT = 0, 1
MASK_SAME = {16: 2, 32: 3, 64: 4}
MASK_OFF = {16: 5, 32: 6, 64: 7}

C_QK = 0
C_MV = C_QK + 2 * D_MLSTM
C_MO = C_MV + D_MLSTM
C_MG = C_MO + D_MLSTM
C_RW = C_MG + 128
D_RW_IN = 3 * D_RWKV + DECAY_LORA + ICLR_LORA + GATE_LORA
C_END = C_RW + D_RW_IN

VMEM_LIMIT = 56 * 1024 * 1024


def _bdot(a, b):
    return jnp.dot(a.astype(BF16), b.astype(BF16), preferred_element_type=F32)


def _bdot_nt(a, b):
    return lax.dot_general(a.astype(BF16), b.astype(BF16), (((1,), (1,)), ((), ())),
                           preferred_element_type=F32)


def _bdot_tn(a, b):
    return lax.dot_general(a.astype(BF16), b.astype(BF16), (((0,), (0,)), ((), ())),
                           preferred_element_type=F32)


def _exact_left_dot(ones_mat, x, passes=3):
    acc, rest = None, x
    for i in range(passes):
        piece = rest.astype(BF16)
        part = jnp.dot(ones_mat, piece, preferred_element_type=F32)
        acc = part if acc is None else acc + part
        if i + 1 < passes:
            rest = rest - piece.astype(F32)
    return acc


def _head_sums(x, seg):
    n, w = x.shape
    hi = x.astype(BF16)
    lo = (x - hi.astype(F32)).astype(BF16)
    stacked = jnp.concatenate([hi[:, :w // 2], hi[:, w // 2:], lo[:, :w // 2], lo[:, w // 2:]], axis=0)
    sums = jnp.dot(stacked, seg, preferred_element_type=F32)
    return jnp.concatenate([sums[0:n] + sums[2 * n:3 * n], sums[n:2 * n] + sums[3 * n:4 * n]], axis=1)


def _rms(x, g):
    return x * lax.rsqrt(jnp.mean(x * x, axis=-1, keepdims=True) + EPS) * g


def _sigmoid(x):
    return 1.0 / (1.0 + jnp.exp(-x))


def _log_sigmoid(x):
    return jnp.minimum(x, 0.0) - jnp.log(1.0 + jnp.exp(-jnp.abs(x)))


def _mem_kv_kernel(m_ref, g_ref, w_ref, kv_ref):
    u = _rms(m_ref[0], g_ref[...]).astype(BF16)
    kv_ref[0] = jnp.dot(u, w_ref[...], preferred_element_type=F32).astype(BF16)


def _mem_kv(mem, g, w):
    b, m, d = mem.shape
    c = w.shape[1]
    return pl.pallas_call(
        _mem_kv_kernel,
        grid=(b,),
        in_specs=[pl.BlockSpec((1, m, d), lambda i: (i, 0, 0)),
                  pl.BlockSpec((1, d), lambda i: (0, 0)),
                  pl.BlockSpec((d, c), lambda i: (0, 0), pipeline_mode=pl.Buffered(1))],
        out_specs=pl.BlockSpec((1, m, c), lambda i: (i, 0, 0)),
        out_shape=jax.ShapeDtypeStruct((b, m, c), BF16),
        compiler_params=pltpu.CompilerParams(dimension_semantics=("parallel",),
                                             vmem_limit_bytes=VMEM_LIMIT),
        name="mem_kv",
    )(mem, g, w)


def _interleaved(*gens, filler=None):
    results = [None] * len(gens)
    live = list(enumerate(gens))
    while live:
        still, hint = [], 0
        for i, g in live:
            try:
                hint = max(hint, next(g) or 0)
                still.append((i, g))
            except StopIteration as stop:
                results[i] = stop.value
        live = still
        if filler is not None:
            for _ in range(hint):
                next(filler, None)
        yield hint
    if filler is not None:
        for _ in filler:
            pass
    return results


def _run(gen):
    try:
        while True:
            next(gen)
    except StopIteration as stop:
        return stop.value


def _fold_rows(x, s):
    acc = x[0:s]
    for j in range(1, x.shape[0] // s):
        acc = acc + x[j * s:(j + 1) * s]
    return acc


def _tile_rows(xc, n):
    return jnp.concatenate([xc] * (n // xc.shape[0]), axis=0)


def _unit_lower_inverses(ms, same_blk, off_blk):
    n = ms[0].shape[0]
    b = INV_BASE
    dot = functools.partial(jnp.dot, preferred_element_type=F32)
    lane = lax.broadcasted_iota(jnp.int32, (b, 2 * n), 1)
    eye_c = (lane % b == lax.broadcasted_iota(jnp.int32, (b, 2 * n), 0)).astype(F32)
    zeros = jnp.zeros((n, n), BF16)
    pairs = [(ms[j], ms[j + 1]) for j in range(0, len(ms), 2)]

    def block_diag2(lo, hi):
        return jnp.concatenate([jnp.concatenate([lo, zeros], axis=1),
                                jnp.concatenate([zeros, hi], axis=1)], axis=0)

    def fold2(pair, mask, s):
        return jnp.concatenate([_fold_rows(m * mask, s) for m in pair], axis=1)

    def expand2(xc, mask):
        return block_diag2(_tile_rows(xc[:, :n].astype(BF16), n) * mask,
                           _tile_rows(xc[:, n:].astype(BF16), n) * mask)

    m_diag = [(lo * same_blk[b], hi * same_blk[b]) for lo, hi in pairs]
    pcs = [jnp.concatenate([_fold_rows(lo, b), _fold_rows(hi, b)], axis=1) for lo, hi in m_diag]
    tcs = [eye_c + p.astype(F32) for p in pcs]
    pcs = [dot(p, block_diag2(lo, hi)) for p, (lo, hi) in zip(pcs, m_diag)]
    yield
    s = 2
    while 2 * s < b:
        xs = [dot(jnp.concatenate([t, p], axis=0).astype(BF16), expand2(p, same_blk[b]))
              for t, p in zip(tcs, pcs)]
        tcs = [t + x[:b] for t, x in zip(tcs, xs)]
        pcs = [x[b:] for x in xs]
        s *= 2
        yield
    tcs = [t + dot(t.astype(BF16), expand2(p, same_blk[b])) for t, p in zip(tcs, pcs)]
    yield
    s = b
    while s < n:
        ccs = [fold2(pr, off_blk[s], s) for pr in pairs]
        x1s = [dot(c, expand2(t, same_blk[s])) for c, t in zip(ccs, tcs)]
        yield
        x2s = [dot(t.astype(BF16), expand2(x, off_blk[s])) for t, x in zip(tcs, x1s)]
        even = (lax.broadcasted_iota(jnp.int32, (s, 2 * n), 1) // s) % 2 == 0
        tcs = [jnp.concatenate([jnp.where(even, t, 0.0), x + jnp.where(even, 0.0, t)], axis=0)
               for t, x in zip(tcs, x2s)]
        yield
        s *= 2
    return tcs


def _mlstm_chunk(zvo, qkbuf, conv_ref, gbias_ref, mnorm_ref, states, tri, lower):
    L = CHUNK
    conv_w = conv_ref[...]
    qk = qkbuf[8:8 + L, :] * conv_w[CONV_WIDTH - 1:CONV_WIDTH, :]
    for j in range(1, CONV_WIDTH):
        qk += qkbuf[8 - j:8 - j + L, :] * conv_w[CONV_WIDTH - 1 - j:CONV_WIDTH - j, :]
    yield
    qk = qk * _sigmoid(qk)
    q_all = qk[:, :D_MLSTM]
    k_all = qk[:, D_MLSTM:] * (MLSTM_HEAD_DIM ** -0.5)
    yield

    gates = zvo[:, C_MG - C_MV:C_MG - C_MV + 128] + gbias_ref[...]
    bcum = _exact_left_dot(tri, _log_sigmoid(gates))
    gates_t = gates.T
    bcum_t = bcum.T

    outs, new_states = [], []
    for h in range(MLSTM_HEADS):
        hs = slice(h * MLSTM_HEAD_DIM, (h + 1) * MLSTM_HEAD_DIM)
        qh = q_all[:, hs].astype(BF16)
        kh = k_all[:, hs]
        vh = zvo[:, h * MLSTM_HEAD_DIM:(h + 1) * MLSTM_HEAD_DIM]
        oh = zvo[:, C_MO - C_MV + h * MLSTM_HEAD_DIM:C_MO - C_MV + (h + 1) * MLSTM_HEAD_DIM]
        f_lane = MLSTM_HEADS + h
        b_col = bcum[:, f_lane:f_lane + 1]
        b_row = bcum_t[f_lane:f_lane + 1, :]
        ig_col = gates[:, h:h + 1]
        ig_row = gates_t[h:h + 1, :]
        g_tot = bcum[L - 1:L, f_lane:f_lane + 1]
        ct_prev, n_prev, m_prev = states[h]
        m_prev = m_prev[:, 0:1]

        dm = jnp.where(lower, b_col - b_row + ig_row, -jnp.inf)
        m_intra = jnp.max(dm, axis=-1, keepdims=True)
        m_inter = b_col + m_prev
        m_t = jnp.maximum(m_inter, m_intra)
        s_mat = _bdot_nt(qh, kh) * jnp.exp(dm - m_t)
        s_inter = jnp.exp(m_inter - m_t)
        num = _bdot(s_mat, vh) + s_inter * _bdot(qh, ct_prev)
        den = (jnp.sum(s_mat, axis=-1, keepdims=True)
               + s_inter * jnp.sum(q_all[:, hs] * n_prev, axis=-1, keepdims=True))
        hh = num / jnp.maximum(jnp.abs(den), jnp.exp(-m_t))
        yield

        a_col = g_tot - b_col + ig_col
        m_loc = jnp.max(a_col, axis=0, keepdims=True)
        wgt = jnp.exp(a_col - m_loc)
        d_ct = _bdot_tn(kh, vh * wgt)
        d_n = jnp.sum(kh * wgt, axis=0, keepdims=True)
        m_new = jnp.maximum(g_tot + m_prev, m_loc)
        s_old = jnp.exp(g_tot + m_prev - m_new)
        s_new = jnp.exp(m_loc - m_new)
        new_states.append((s_old * ct_prev + s_new * d_ct, s_old * n_prev + s_new * d_n,
                           jnp.broadcast_to(m_new, (1, 128))))

        hh = hh * lax.rsqrt(jnp.mean(hh * hh, axis=-1, keepdims=True) + EPS)
        outs.append(hh * mnorm_ref[:, hs] * _sigmoid(oh))
        yield
    return outs, new_states


def _project_next(x_ref, g_ref, w_ref, znext):
    u = _rms(x_ref[...].reshape(STREAMS * CHUNK, D_MODEL), g_ref[...]).astype(BF16)
    yield
    for c0 in range(0, C_END, PROJ_COLS):
        c1 = min(c0 + PROJ_COLS, C_END)
        znext[:, c0:c1] = jnp.dot(u, w_ref[:, c0:c1], preferred_element_type=F32)
        yield


def _chunk_stream(zvo, qkbuf, rwbuf, s_prev, mlstm_states, prm, masks_ref):
    L = CHUNK
    pairs = range(RWKV_PAIRS)
    tri = masks_ref[MASK_LOWER]
    lower = tri > 0
    strict = masks_ref[MASK_STRICT] > 0
    same_blk = {s: masks_ref[MASK_SAME[s]] for s in MASK_SAME}
    off_blk = {s: masks_ref[MASK_OFF[s]] for s in MASK_OFF}

    z_now = rwbuf[8:8 + L, :]
    z_prev = rwbuf[7:7 + L, :]
    zm = z_now + (z_prev - z_now) * prm["mu"][...]
    yield 2
    r = zm[:, 0:D_RWKV]
    k = zm[:, D_RWKV:2 * D_RWKV]
    v = zm[:, 2 * D_RWKV:3 * D_RWKV]
    xwa = zm[:, 3 * D_RWKV:3 * D_RWKV + 128]
    xg = zm[:, 3 * D_RWKV + 128:3 * D_RWKV + 256]
    lane128 = lax.broadcasted_iota(jnp.int32, (L, 128), 1)
    lora_in = jnp.where(lane128 < DECAY_LORA, jnp.tanh(xwa), xwa)
    yield
    lora = _bdot(lora_in, prm["wa"][...])
    logw = -math.exp(-0.5) * _sigmoid(prm["w0"][...] + lora[:, :D_RWKV])
    yield 1
    iclr = _sigmoid(prm["a0"][...] + lora[:, D_RWKV:])
    gate = _bdot(_sigmoid(xg), prm["gup"][...])
    yield 1

    seg = prm["seg"][...]
    kk = k * prm["kk"][...]
    k2 = k * (1.0 + (iclr - 1.0) * prm["ka"][...])
    yield
    seg_sums = _head_sums(jnp.concatenate([kk * kk, r * k2 * prm["rk"][...]], axis=0), seg)
    yield 1
    kk = kk * lax.rsqrt(jnp.maximum(seg_sums[:L], 1e-24))
    bonus = seg_sums[L:]
    a_vec = -kk
    b_vec = kk * iclr
    yield 1

    cum = _exact_left_dot(tri, logw, passes=2)
    c_ref = cum[L // 2 - 1:L // 2, :]
    c_end = cum[L - 1:L, :]
    yield 1
    d_ref = cum - c_ref
    e_pos = jnp.exp(d_ref)
    e_neg = 1.0 / e_pos
    yield 1
    r_rel = r * e_pos
    a_rel = a_vec * jnp.exp(d_ref - logw)
    k_rel = k2 * e_neg
    b_rel = b_vec * e_neg
    e_ref = jnp.exp(c_ref)
    e_tail = jnp.exp(c_end - c_ref)
    e_end = jnp.exp(c_end)
    yield 1

    lane_lo = lane128 < RWKV_HEAD_DIM
    lane_lo2 = lax.broadcasted_iota(jnp.int32, (2 * L, 128), 1) < RWKV_HEAD_DIM
    blockdiag = same_blk[RWKV_HEAD_DIM].astype(F32)

    pair_slices = [slice(p * 128, (p + 1) * 128) for p in pairs]
    m_ab, m_ak, m_r = [], [], []
    ar_rel, bk_rel = [], []
    zero = jnp.zeros((), BF16)
    mlstm = _mlstm_chunk(zvo, qkbuf, prm["conv"], prm["gbias"], prm["mnorm"], mlstm_states, tri, lower)
    for i, ps in enumerate(pair_slices):
        if i < MLSTM_EARLY_ROUNDS:
            next(mlstm)
        ar_rel.append(jnp.concatenate([a_rel[:, ps], r_rel[:, ps]], axis=0).astype(BF16))
        bk_rel.append(jnp.concatenate([b_rel[:, ps], k_rel[:, ps]], axis=0).astype(BF16))
        ar_split = jnp.concatenate([jnp.where(lane_lo2, ar_rel[-1], zero),
                                    jnp.where(lane_lo2, zero, ar_rel[-1])], axis=0)
        g_both = _bdot_nt(ar_split, bk_rel[-1]).astype(BF16)
        for half in range(2):
            g_mat = g_both[2 * L * half:2 * L * (half + 1)]
            m_ab.append(jnp.where(strict, g_mat[:L, :L], zero))
            m_ak.append(jnp.where(strict, g_mat[:L, L:], zero))
            m_r.append(jnp.concatenate([jnp.where(lower, g_mat[L:, :L], zero),
                                        jnp.where(lower, g_mat[L:, L:], zero)], axis=1))
        yield
    t_inv, (m_outs, m_new) = yield from _interleaved(
        _unit_lower_inverses(m_ab, same_blk, off_blk), mlstm)

    def both_heads(stacked):
        return jnp.where(lane_lo, stacked[:L], stacked[L:])

    def two_pairs(lhs0, lhs1, rhs0, rhs1, rhs_transposed=False):
        z = jnp.zeros((128, 128), BF16)
        rhs = jnp.concatenate([jnp.concatenate([rhs0.astype(BF16), z], axis=1),
                               jnp.concatenate([z, rhs1.astype(BF16)], axis=1)], axis=0)
        lhs = jnp.concatenate([lhs0.astype(BF16), lhs1.astype(BF16)], axis=1)
        out = _bdot_nt(lhs, rhs) if rhs_transposed else jnp.dot(lhs, rhs, preferred_element_type=F32)
        return out[:, :128], out[:, 128:]

    v_ps = [v[:, ps].astype(BF16) for ps in pair_slices]
    s_at_ref = [s_prev[p] * e_ref[:, ps] for p, ps in enumerate(pair_slices)]
    inter, rhs = [None] * RWKV_PAIRS, [None] * RWKV_PAIRS
    for p in range(0, RWKV_PAIRS, 2):
        inter[p], inter[p + 1] = two_pairs(ar_rel[p], ar_rel[p + 1], s_at_ref[p], s_at_ref[p + 1],
                                           rhs_transposed=True)
        mv0, mv1 = two_pairs(jnp.concatenate([m_ak[2 * p], m_ak[2 * p + 1]], axis=0),
                             jnp.concatenate([m_ak[2 * p + 2], m_ak[2 * p + 3]], axis=0), v_ps[p], v_ps[p + 1])
        rhs[p] = inter[p][:L] + both_heads(mv0)
        rhs[p + 1] = inter[p + 1][:L] + both_heads(mv1)
    yield
    u_ps = [None] * RWKV_PAIRS
    for p in range(0, RWKV_PAIRS, 2):
        t0, t1 = t_inv[p], t_inv[p + 1]
        u0, u1 = two_pairs(jnp.concatenate([t0[:, :L], t0[:, L:]], axis=0),
                           jnp.concatenate([t1[:, :L], t1[:, L:]], axis=0), rhs[p], rhs[p + 1])
        u_ps[p], u_ps[p + 1] = both_heads(u0), both_heads(u1)
    uv = [jnp.concatenate([u_ps[p].astype(BF16), v_ps[p]], axis=0) for p in pairs]
    yield
    y_parts = [inter[p][L:] + both_heads(_bdot(jnp.concatenate([m_r[2 * p], m_r[2 * p + 1]], axis=0), uv[p]))
               for p in pairs]
    s_new = [s_prev[p] * e_end[:, ps] + (blockdiag * e_tail[:, ps]) * _bdot_tn(uv[p], bk_rel[p])
             for p, ps in enumerate(pair_slices)]
    yield

    y = jnp.concatenate(y_parts, axis=1)
    inv_n = 1.0 / RWKV_HEAD_DIM
    mean = _head_sums(y, seg) * inv_n
    yield 4
    yc = y - mean
    var = _head_sums(yc * yc, seg) * inv_n
    yield 4
    y = yc * lax.rsqrt(var + RWKV_LN_EPS) * prm["lnw"][...] + prm["lnb"][...]
    y = (y + bonus * v) * gate
    return m_outs, m_new, y, s_new


MIXER_PARAMS = ("conv", "gbias", "mnorm", "mu", "w0", "a0", "wa", "gup", "kk", "ka", "rk", "lnw", "lnb", "seg")


def _mixer_kernel(xn_ref, x0_ref, gin_ref, win_ref, *rest):
    prm = dict(zip(MIXER_PARAMS, rest[:len(MIXER_PARAMS)]))
    masks_ref, y_ref, znext, zvo, qkbuf, rwbuf, ct_ref, n_ref, m_ref, s_ref = rest[len(MIXER_PARAMS):]
    L = CHUNK
    t_idx = pl.program_id(1)

    @pl.when((t_idx == 0) & (pl.program_id(0) == 0))
    def _():
        u0 = _rms(x0_ref[...].reshape(STREAMS * L, D_MODEL), gin_ref[...]).astype(BF16)
        znext[...] = jnp.dot(u0, win_ref[...], preferred_element_type=F32)

    @pl.when(t_idx == 0)
    def _():
        qkbuf[...] = jnp.zeros_like(qkbuf)
        rwbuf[...] = jnp.zeros_like(rwbuf)
        ct_ref[...] = jnp.zeros_like(ct_ref)
        n_ref[...] = jnp.zeros_like(n_ref)
        m_ref[...] = jnp.zeros_like(m_ref)
        s_ref[...] = jnp.zeros_like(s_ref)

    @pl.when(t_idx > 0)
    def _():
        for q in range(STREAMS):
            qkbuf[q, 0:8, :] = qkbuf[q, L:L + 8, :]
            rwbuf[q, 0:8, :] = rwbuf[q, L:L + 8, :]

    for q in range(STREAMS):
        rows = slice(q * L, (q + 1) * L)
        qkbuf[q, 8:8 + L, :] = znext[rows, C_QK:C_QK + 2 * D_MLSTM]
        zvo[q] = znext[rows, C_MV:C_RW]
        rwbuf[q, 8:8 + L, :] = znext[rows, C_RW:C_END]

    streams = []
    for q in range(STREAMS):
        s_prev = [s_ref[q, p] for p in range(RWKV_PAIRS)]
        mlstm_states = [(ct_ref[q, h], n_ref[q, h], m_ref[q, h]) for h in range(MLSTM_HEADS)]
        streams.append(_chunk_stream(zvo.at[q], qkbuf.at[q], rwbuf.at[q], s_prev, mlstm_states, prm, masks_ref))

    results = _run(_interleaved(*streams, filler=_project_next(xn_ref, gin_ref, win_ref, znext)))

    for q, (m_outs, m_new, y, s_new) in enumerate(results):
        for h in range(MLSTM_HEADS):
            y_ref[q, :, h * MLSTM_HEAD_DIM:(h + 1) * MLSTM_HEAD_DIM] = m_outs[h].astype(y_ref.dtype)
            ct_ref[q, h], n_ref[q, h], m_ref[q, h] = m_new[h]
        y_ref[q, :, D_MLSTM:] = y.astype(y_ref.dtype)
        for p in range(RWKV_PAIRS):
            s_ref[q, p] = s_new[p]


def _mixer(x, gin, w_in, mixer_params, masks):
    b, t, d = x.shape
    nt = t // CHUNK
    const = lambda a: pl.BlockSpec(a.shape, lambda i, j: (0,) * a.ndim, pipeline_mode=pl.Buffered(1))
    params = (gin, w_in) + tuple(mixer_params) + (masks,)
    blk = (STREAMS, CHUNK, d)
    groups = b // STREAMS

    def next_chunk(i, j):
        wrap = j + 1 == nt
        return (jnp.where(wrap, jnp.minimum(i + 1, groups - 1), i), jnp.where(wrap, 0, j + 1), 0)

    return pl.pallas_call(
        _mixer_kernel,
        grid=(b // STREAMS, nt),
        in_specs=[pl.BlockSpec(blk, next_chunk),
                  pl.BlockSpec(blk, lambda i, j: (0, 0, 0), pipeline_mode=pl.Buffered(1))]
                 + [const(a) for a in params],
        out_specs=pl.BlockSpec(blk, lambda i, j: (i, j, 0)),
        out_shape=jax.ShapeDtypeStruct((b, t, d), BF16),
        scratch_shapes=[
            pltpu.VMEM((STREAMS * CHUNK, C_END), F32),
            pltpu.VMEM((STREAMS, CHUNK, C_RW - C_MV), F32),
            pltpu.VMEM((STREAMS, CHUNK + 8, 2 * D_MLSTM), F32),
            pltpu.VMEM((STREAMS, CHUNK + 8, D_RW_IN), F32),
            pltpu.VMEM((STREAMS, MLSTM_HEADS, MLSTM_HEAD_DIM, MLSTM_HEAD_DIM), F32),
            pltpu.VMEM((STREAMS, MLSTM_HEADS, 1, MLSTM_HEAD_DIM), F32),
            pltpu.VMEM((STREAMS, MLSTM_HEADS, 1, 128), F32),
            pltpu.VMEM((STREAMS, RWKV_PAIRS, 128, 128), F32),
        ],
        compiler_params=pltpu.CompilerParams(dimension_semantics=("arbitrary", "arbitrary"),
                                             vmem_limit_bytes=VMEM_LIMIT),
        name="mixer",
    )(x, x, *params)


def _tail_kernel(x_ref, y_ref, kv_ref, wmix_ref, gx_ref, wq_ref, wo_ref, gf_ref, wg_ref, wu_ref, wd_ref,
                 gfin_ref, o_ref):
    h = x_ref[...] + jnp.dot(y_ref[...], wmix_ref[...], preferred_element_type=F32)

    q = _bdot(_rms(h, gx_ref[...]), wq_ref[...]).astype(BF16)
    heads = []
    for i in range(XATTN_HEADS):
        hs = slice(i * XATTN_HEAD_DIM, (i + 1) * XATTN_HEAD_DIM)
        k_h = kv_ref[0, :, i * XATTN_HEAD_DIM:(i + 1) * XATTN_HEAD_DIM]
        v_h = kv_ref[0, :, D_MODEL + i * XATTN_HEAD_DIM:D_MODEL + (i + 1) * XATTN_HEAD_DIM]
        s = _bdot_nt(q[:, hs], k_h) * (XATTN_HEAD_DIM ** -0.5)
        e = jnp.exp(s - jnp.max(s, axis=-1, keepdims=True))
        heads.append((_bdot(e, v_h) / jnp.sum(e, axis=-1, keepdims=True)).astype(BF16))
    h = h + jnp.dot(jnp.concatenate(heads, axis=1), wo_ref[...], preferred_element_type=F32)

    u = _rms(h, gf_ref[...]).astype(BF16)
    for c0 in range(0, wg_ref.shape[1], FF_COLS):
        gate = jnp.dot(u, wg_ref[:, c0:c0 + FF_COLS], preferred_element_type=F32)
        up = jnp.dot(u, wu_ref[:, c0:c0 + FF_COLS], preferred_element_type=F32)
        act = (gate * _sigmoid(gate) * up).astype(BF16)
        h = h + jnp.dot(act, wd_ref[c0:c0 + FF_COLS, :], preferred_element_type=F32)

    o_ref[...] = _rms(h, gfin_ref[...])


def _tail(x2, y2, kv, tail_params, tm, t):
    n, d = x2.shape
    per_b = t // tm
    const = lambda a: pl.BlockSpec(a.shape, lambda i: (0,) * a.ndim, pipeline_mode=pl.Buffered(1))
    return pl.pallas_call(
        _tail_kernel,
        grid=(n // tm,),
        in_specs=[pl.BlockSpec((tm, d), lambda i: (i, 0)),
                  pl.BlockSpec((tm, d), lambda i: (i, 0)),
                  pl.BlockSpec((1,) + kv.shape[1:], lambda i: (i // per_b, 0, 0))]
                 + [const(a) for a in tail_params],
        out_specs=pl.BlockSpec((tm, d), lambda i: (i, 0)),
        out_shape=jax.ShapeDtypeStruct((n, d), F32),
        compiler_params=pltpu.CompilerParams(dimension_semantics=("parallel",),
                                             vmem_limit_bytes=VMEM_LIMIT),
        name="tail",
    )(x2, y2, kv, *tail_params)


def _chunk_masks():
    i = jnp.arange(CHUNK)
    row, col = i[:, None], i[None, :]
    planes = [None] * 8
    planes[MASK_LOWER] = row >= col
    planes[MASK_STRICT] = row > col
    for s, idx in MASK_SAME.items():
        planes[idx] = (row // s) == (col // s)
    for s, idx in MASK_OFF.items():
        planes[idx] = ((row // (2 * s)) == (col // (2 * s))) & ((row // s) > (col // s))
    return jnp.stack(planes).astype(BF16)


def _regroup_w_in(w):
    n_m = 4 * D_MLSTM
    gates = w[:, n_m:n_m + 2 * MLSTM_HEADS]
    gates = jnp.pad(gates, ((0, 0), (0, 128 - 2 * MLSTM_HEADS)))
    return jnp.concatenate([w[:, :n_m], gates, w[:, n_m + 2 * MLSTM_HEADS:]], axis=1).astype(BF16)


def kernel(x, mem, norm_mix, w_in, mlstm_conv, mlstm_i_bias, mlstm_f_bias, mlstm_norm, rwkv_mu, rwkv_w0, rwkv_w_up, rwkv_a0, rwkv_a_up, rwkv_g_up, rwkv_k_k, rwkv_k_a, rwkv_r_k, rwkv_ln_w, rwkv_ln_b, w_mix_out, norm_xattn, norm_mem, xattn_wq, xattn_wkv, xattn_wo, norm_ffn, ffn_w_gate, ffn_w_up, ffn_w_down, norm_final):
    b, t, d = x.shape
    assert d == D_MODEL and t % CHUNK == 0 and b % STREAMS == 0 and norm_mix.shape[0] == 1
    row = lambda a: a.reshape(1, -1).astype(F32)

    kv = _mem_kv(mem, row(norm_mem[0]), xattn_wkv[0].astype(BF16))

    gbias = jnp.pad(jnp.concatenate([mlstm_i_bias[0], mlstm_f_bias[0]]), (0, 128 - 2 * MLSTM_HEADS))
    zeros = jnp.zeros((DECAY_LORA, D_RWKV), F32)
    wa = jnp.concatenate([jnp.concatenate([rwkv_w_up[0], zeros], axis=1),
                          jnp.concatenate([zeros, rwkv_a_up[0]], axis=1)], axis=0).astype(BF16)
    head_id = jnp.arange(D_RWKV // 2) // RWKV_HEAD_DIM
    seg = (head_id[:, None] == head_id[None, :]).astype(BF16)
    mixer_params = (mlstm_conv[0], row(gbias), row(mlstm_norm[0]), row(rwkv_mu[0]), row(rwkv_w0[0]),
                    row(rwkv_a0[0]), wa, rwkv_g_up[0].astype(BF16), row(rwkv_k_k[0]), row(rwkv_k_a[0]),
                    row(rwkv_r_k[0]), row(rwkv_ln_w[0]), row(rwkv_ln_b[0]), seg)
    tail_params = (w_mix_out[0].astype(BF16), row(norm_xattn[0]), xattn_wq[0].astype(BF16),
                   xattn_wo[0].astype(BF16), row(norm_ffn[0]), ffn_w_gate[0].astype(BF16),
                   ffn_w_up[0].astype(BF16), ffn_w_down[0].astype(BF16), row(norm_final))
    y = _mixer(x, row(norm_mix[0]), _regroup_w_in(w_in[0]), mixer_params, _chunk_masks())
    tm = TAIL_ROWS if t % TAIL_ROWS == 0 else CHUNK
    out = _tail(x.reshape(b * t, d), y.reshape(b * t, d), kv, tail_params, tm, t)
    return out.reshape(b, t, d)
```

```python
import functools
import math

import jax
import jax.numpy as jnp
from jax import lax
from jax.experimental import pallas as pl
from jax.experimental.pallas import tpu as pltpu

F32 = jnp.float32
BF16 = jnp.bfloat16

EPS = 1e-6
RWKV_LN_EPS = 64e-5
CONV_WIDTH = 4

D_MODEL = 1024
D_MLSTM = 512
D_RWKV = 512
MLSTM_HEADS = 4
MLSTM_HEAD_DIM = 128
RWKV_HEAD_DIM = 64
RWKV_PAIRS = D_RWKV // (2 * RWKV_HEAD_DIM)
DECAY_LORA = 64
ICLR_LORA = 64
GATE_LORA = 128
XATTN_HEADS = 4
XATTN_HEAD_DIM = 256
CHUNK = 128
INV_BASE = 16
PROJ_COLS = 256
MLSTM_EARLY_ROUNDS = 2
STREAMS = 2
FF_COLS = 256
TAIL_ROWS = 512

MASK_LOWER, MASK_STRICT = 0, 1
MASK_SAME = {16: 2, 32: 3, 64: 4}
MASK_OFF = {16: 5, 32: 6, 64: 7}

C_QK = 0
C_MV = C_QK + 2 * D_MLSTM
C_MO = C_MV + D_MLSTM
C_MG = C_MO + D_MLSTM
C_RW = C_MG + 128
D_RW_IN = 3 * D_RWKV + DECAY_LORA + ICLR_LORA + GATE_LORA
C_END = C_RW + D_RW_IN

VMEM_LIMIT = 56 * 1024 * 1024


def _bdot(a, b):
    return jnp.dot(a.astype(BF16), b.astype(BF16), preferred_element_type=F32)


def _bdot_nt(a, b):
    return lax.dot_general(a.astype(BF16), b.astype(BF16), (((1,), (1,)), ((), ())),
                           preferred_element_type=F32)


def _bdot_tn(a, b):
    return lax.dot_general(a.astype(BF16), b.astype(BF16), (((0,), (0,)), ((), ())),
                           preferred_element_type=F32)


def _exact_left_dot(ones_mat, x, passes=3):
    acc, rest = None, x
    for i in range(passes):
        piece = rest.astype(BF16)
        part = jnp.dot(ones_mat, piece, preferred_element_type=F32)
        acc = part if acc is None else acc + part
        if i + 1 < passes:
            rest = rest - piece.astype(F32)
    return acc


def _head_sums(x, seg):
    n, w = x.shape
    hi = x.astype(BF16)
    lo = (x - hi.astype(F32)).astype(BF16)
    stacked = jnp.concatenate([hi[:, :w // 2], hi[:, w // 2:], lo[:, :w // 2], lo[:, w // 2:]], axis=0)
    sums = jnp.dot(stacked, seg, preferred_element_type=F32)
    return jnp.concatenate([sums[0:n] + sums[2 * n:3 * n], sums[n:2 * n] + sums[3 * n:4 * n]], axis=1)


def _rms(x, g):
    return x * lax.rsqrt(jnp.mean(x * x, axis=-1, keepdims=True) + EPS) * g


def _sigmoid(x):
    return 1.0 / (1.0 + jnp.exp(-x))


def _log_sigmoid(x):
    return jnp.minimum(x, 0.0) - jnp.log(1.0 + jnp.exp(-jnp.abs(x)))


def _mem_kv_kernel(m_ref, g_ref, w_ref, kv_ref):
    u = _rms(m_ref[0], g_ref[...]).astype(BF16)
    kv_ref[0] = jnp.dot(u, w_ref[...], preferred_element_type=F32).astype(BF16)


def _mem_kv(mem, g, w):
    b, m, d = mem.shape
    c = w.shape[1]
    return pl.pallas_call(
        _mem_kv_kernel,
        grid=(b,),
        in_specs=[pl.BlockSpec((1, m, d), lambda i: (i, 0, 0)),
                  pl.BlockSpec((1, d), lambda i: (0, 0)),
                  pl.BlockSpec((d, c), lambda i: (0, 0), pipeline_mode=pl.Buffered(1))],
        out_specs=pl.BlockSpec((1, m, c), lambda i: (i, 0, 0)),
        out_shape=jax.ShapeDtypeStruct((b, m, c), BF16),
        compiler_params=pltpu.CompilerParams(dimension_semantics=("parallel",),
                                             vmem_limit_bytes=VMEM_LIMIT),
        name="mem_kv",
    )(mem, g, w)


def _interleaved(*gens, filler=None):
    results = [None] * len(gens)
    live = list(enumerate(gens))
    while live:
        still, hint = [], 0
        for i, g in live:
            try:
                hint = max(hint, next(g) or 0)
                still.append((i, g))
            except StopIteration as stop:
                results[i] = stop.value
        live = still
        if filler is not None:
            for _ in range(hint):
                next(filler, None)
        yield hint
    if filler is not None:
        for _ in filler:
            pass
    return results


def _run(gen):
    try:
        while True:
            next(gen)
    except StopIteration as stop:
        return stop.value


def _fold_rows(x, s):
    acc = x[0:s]
    for j in range(1, x.shape[0] // s):
        acc = acc + x[j * s:(j + 1) * s]
    return acc


def _tile_rows(xc, n):
    return jnp.concatenate([xc] * (n // xc.shape[0]), axis=0)


def _unit_lower_inverses(ms, same_blk, off_blk):
    n = ms[0].shape[0]
    b = INV_BASE
    dot = functools.partial(jnp.dot, preferred_element_type=F32)
    lane = lax.broadcasted_iota(jnp.int32, (b, 2 * n), 1)
    eye_c = (lane % b == lax.broadcasted_iota(jnp.int32, (b, 2 * n), 0)).astype(F32)
    zeros = jnp.zeros((n, n), BF16)
    pairs = [(ms[j], ms[j + 1]) for j in range(0, len(ms), 2)]

    def block_diag2(lo, hi):
        return jnp.concatenate([jnp.concatenate([lo, zeros], axis=1),
                                jnp.concatenate([zeros, hi], axis=1)], axis=0)

    def fold2(pair, mask, s):
        return jnp.concatenate([_fold_rows(m * mask, s) for m in pair], axis=1)

    def expand2(xc, mask):
        return block_diag2(_tile_rows(xc[:, :n].astype(BF16), n) * mask,
                           _tile_rows(xc[:, n:].astype(BF16), n) * mask)

    def mm(lhs, rhs_c, mask):
        rows = lhs.shape[0]
        r_hi = rhs_c.astype(BF16)
        r_lo = (rhs_c - r_hi.astype(F32)).astype(BF16)
        if lhs.dtype == BF16:
            return dot(lhs, expand2(r_hi, mask)) + dot(lhs, expand2(r_lo, mask))
        l_hi = lhs.astype(BF16)
        l_lo = (lhs - l_hi.astype(F32)).astype(BF16)
        both = dot(jnp.concatenate([l_hi, l_lo], axis=0), expand2(r_hi, mask))
        return both[:rows] + both[rows:] + dot(l_hi, expand2(r_lo, mask))

    m_diag = [(lo * same_blk[b], hi * same_blk[b]) for lo, hi in pairs]
    pcs = [jnp.concatenate([_fold_rows(lo, b), _fold_rows(hi, b)], axis=1) for lo, hi in m_diag]
    tcs = [eye_c + p.astype(F32) for p in pcs]
    pcs = [dot(p, block_diag2(lo, hi)) for p, (lo, hi) in zip(pcs, m_diag)]
    yield
    s = 2
    while 2 * s < b:
        xs = [mm(jnp.concatenate([t, p], axis=0), p, same_blk[b]) for t, p in zip(tcs, pcs)]
        tcs = [t + x[:b] for t, x in zip(tcs, xs)]
        pcs = [x[b:] for x in xs]
        s *= 2
        yield
    tcs = [t + mm(t, p, same_blk[b]) for t, p in zip(tcs, pcs)]
    yield
    s = b
    while s < n:
        ccs = [fold2(pr, off_blk[s], s) for pr in pairs]
        x1s = [mm(c, t, same_blk[s]) for c, t in zip(ccs, tcs)]
        yield
        x2s = [mm(t, x, off_blk[s]) for t, x in zip(tcs, x1s)]
        even = (lax.broadcasted_iota(jnp.int32, (s, 2 * n), 1) // s) % 2 == 0
        tcs = [jnp.concatenate([jnp.where(even, t, 0.0), x + jnp.where(even, 0.0, t)], axis=0)
               for t, x in zip(tcs, x2s)]
        yield
        s *= 2
    return tcs


def _mlstm_chunk(zvo, qkbuf, conv_ref, gbias_ref, mnorm_ref, states, tri, lower):
    L = CHUNK
    conv_w = conv_ref[...]
    qk = qkbuf[8:8 + L, :] * conv_w[CONV_WIDTH - 1:CONV_WIDTH, :]
    for j in range(1, CONV_WIDTH):
        qk += qkbuf[8 - j:8 - j + L, :] * conv_w[CONV_WIDTH - 1 - j:CONV_WIDTH - j, :]
    yield
    qk = qk * _sigmoid(qk)
    q_all = qk[:, :D_MLSTM]
    k_all = qk[:, D_MLSTM:] * (MLSTM_HEAD_DIM ** -0.5)
    yield

    gates = zvo[:, C_MG - C_MV:C_MG - C_MV + 128] + gbias_ref[...]
    bcum = _exact_left_dot(tri, _log_sigmoid(gates))
    gates_t = gates.T
    bcum_t = bcum.T

    outs, new_states = [], []
    for h in range(MLSTM_HEADS):
        hs = slice(h * MLSTM_HEAD_DIM, (h + 1) * MLSTM_HEAD_DIM)
        qh = q_all[:, hs].astype(BF16)
        kh = k_all[:, hs]
        vh = zvo[:, h * MLSTM_HEAD_DIM:(h + 1) * MLSTM_HEAD_DIM]
        oh = zvo[:, C_MO - C_MV + h * MLSTM_HEAD_DIM:C_MO - C_MV + (h + 1) * MLSTM_HEAD_DIM]
        f_lane = MLSTM_HEADS + h
        b_col = bcum[:, f_lane:f_lane + 1]
        b_row = bcum_t[f_lane:f_lane + 1, :]
        ig_col = gates[:, h:h + 1]
        ig_row = gates_t[h:h + 1, :]
        g_tot = bcum[L - 1:L, f_lane:f_lane + 1]
        ct_prev, n_prev, m_prev = states[h]
        m_prev = m_prev[:, 0:1]

        dm = jnp.where(lower, b_col - b_row + ig_row, -jnp.inf)
        m_intra = jnp.max(dm, axis=-1, keepdims=True)
        m_inter = b_col + m_prev
        m_t = jnp.maximum(m_inter, m_intra)
        s_mat = _bdot_nt(qh, kh) * jnp.exp(dm - m_t)
        s_inter = jnp.exp(m_inter - m_t)
        num = _bdot(s_mat, vh) + s_inter * _bdot(qh, ct_prev)
        den = (jnp.sum(s_mat, axis=-1, keepdims=True)
               + s_inter * jnp.sum(q_all[:, hs] * n_prev, axis=-1, keepdims=True))
        hh = num / jnp.maximum(jnp.abs(den), jnp.exp(-m_t))
        yield

        a_col = g_tot - b_col + ig_col
        m_loc = jnp.max(a_col, axis=0, keepdims=True)
        wgt = jnp.exp(a_col - m_loc)
        d_ct = _bdot_tn(kh, vh * wgt)
        d_n = jnp.sum(kh * wgt, axis=0, keepdims=True)
        m_new = jnp.maximum(g_tot + m_prev, m_loc)
        s_old = jnp.exp(g_tot + m_prev - m_new)
        s_new = jnp.exp(m_loc - m_new)
        new_states.append((s_old * ct_prev + s_new * d_ct, s_old * n_prev + s_new * d_n,
                           jnp.broadcast_to(m_new, (1, 128))))

        hh = hh * lax.rsqrt(jnp.mean(hh * hh, axis=-1, keepdims=True) + EPS)
        outs.append(hh * mnorm_ref[:, hs] * _sigmoid(oh))
        yield
    return outs, new_states


def _project_next(x_ref, g_ref, w_ref, znext):
    u = _rms(x_ref[...].reshape(STREAMS * CHUNK, D_MODEL), g_ref[...]).astype(BF16)
    yield
    for c0 in range(0, C_END, PROJ_COLS):
        c1 = min(c0 + PROJ_COLS, C_END)
        znext[:, c0:c1] = jnp.dot(u, w_ref[:, c0:c1], preferred_element_type=F32)
        yield


def _chunk_stream(zvo, qkbuf, rwbuf, s_prev, mlstm_states, prm, masks_ref):
    L = CHUNK
    pairs = range(RWKV_PAIRS)
    tri = masks_ref[MASK_LOWER]
    lower = tri > 0
    strict = masks_ref[MASK_STRICT] > 0
    same_blk = {s: masks_ref[MASK_SAME[s]] for s in MASK_SAME}
    off_blk = {s: masks_ref[MASK_OFF[s]] for s in MASK_OFF}

    z_now = rwbuf[8:8 + L, :]
    z_prev = rwbuf[7:7 + L, :]
    zm = z_now + (z_prev - z_now) * prm["mu"][...]
    yield 2
    r = zm[:, 0:D_RWKV]
    k = zm[:, D_RWKV:2 * D_RWKV]
    v = zm[:, 2 * D_RWKV:3 * D_RWKV]
    xwa = zm[:, 3 * D_RWKV:3 * D_RWKV + 128]
    xg = zm[:, 3 * D_RWKV + 128:3 * D_RWKV + 256]
    lane128 = lax.broadcasted_iota(jnp.int32, (L, 128), 1)
    lora_in = jnp.where(lane128 < DECAY_LORA, jnp.tanh(xwa), xwa)
    yield
    lora = _bdot(lora_in, prm["wa"][...])
    logw = -math.exp(-0.5) * _sigmoid(prm["w0"][...] + lora[:, :D_RWKV])
    yield 1
    iclr = _sigmoid(prm["a0"][...] + lora[:, D_RWKV:])
    gate = _bdot(_sigmoid(xg), prm["gup"][...])
    yield 1

    seg = prm["seg"][...]
    kk = k * prm["kk"][...]
    k2 = k * (1.0 + (iclr - 1.0) * prm["ka"][...])
    yield
    seg_sums = _head_sums(jnp.concatenate([kk * kk, r * k2 * prm["rk"][...]], axis=0), seg)
    yield 1
    kk = kk * lax.rsqrt(jnp.maximum(seg_sums[:L], 1e-24))
    bonus = seg_sums[L:]
    a_vec = -kk
    b_vec = kk * iclr
    yield 1

    cum = _exact_left_dot(tri, logw, passes=2)
    c_ref = cum[L // 2 - 1:L // 2, :]
    c_end = cum[L - 1:L, :]
    yield 1
    d_ref = cum - c_ref
    e_pos = jnp.exp(d_ref)
    e_neg = 1.0 / e_pos
    yield 1
    r_rel = r * e_pos
    a_rel = a_vec * jnp.exp(d_ref - logw)
    k_rel = k2 * e_neg
    b_rel = b_vec * e_neg
    e_ref = jnp.exp(c_ref)
    e_tail = jnp.exp(c_end - c_ref)
    e_end = jnp.exp(c_end)
    yield 1

    lane_lo = lane128 < RWKV_HEAD_DIM
    lane_lo2 = lax.broadcasted_iota(jnp.int32, (2 * L, 128), 1) < RWKV_HEAD_DIM
    blockdiag = same_blk[RWKV_HEAD_DIM].astype(F32)

    pair_slices = [slice(p * 128, (p + 1) * 128) for p in pairs]
    m_ab, m_ak, m_r = [], [], []
    ar_rel, bk_rel = [], []
    zero = jnp.zeros((), BF16)
    mlstm = _mlstm_chunk(zvo, qkbuf, prm["conv"], prm["gbias"], prm["mnorm"], mlstm_states, tri, lower)
    for i, ps in enumerate(pair_slices):
        if i < MLSTM_EARLY_ROUNDS:
            next(mlstm)
        ar_rel.append(jnp.concatenate([a_rel[:, ps], r_rel[:, ps]], axis=0).astype(BF16))
        bk_rel.append(jnp.concatenate([b_rel[:, ps], k_rel[:, ps]], axis=0).astype(BF16))
        ar_split = jnp.concatenate([jnp.where(lane_lo2, ar_rel[-1], zero),
                                    jnp.where(lane_lo2, zero, ar_rel[-1])], axis=0)
        g_both = _bdot_nt(ar_split, bk_rel[-1]).astype(BF16)
        for half in range(2):
            g_mat = g_both[2 * L * half:2 * L * (half + 1)]
            m_ab.append(jnp.where(strict, g_mat[:L, :L], zero))
            m_ak.append(jnp.where(strict, g_mat[:L, L:], zero))
            m_r.append(jnp.concatenate([jnp.where(lower, g_mat[L:, :L], zero),
                                        jnp.where(lower, g_mat[L:, L:], zero)], axis=1))
        yield
    t_inv, (m_outs, m_new) = yield from _interleaved(
        _unit_lower_inverses(m_ab, same_blk, off_blk), mlstm)

    def both_heads(stacked):
        return jnp.where(lane_lo, stacked[:L], stacked[L:])

    def two_pairs(lhs0, lhs1, rhs0, rhs1, rhs_transposed=False):
        z = jnp.zeros((128, 128), BF16)
        rhs = jnp.concatenate([jnp.concatenate([rhs0.astype(BF16), z], axis=1),
                               jnp.concatenate([z, rhs1.astype(BF16)], axis=1)], axis=0)
        lhs = jnp.concatenate([lhs0.astype(BF16), lhs1.astype(BF16)], axis=1)
        out = _bdot_nt(lhs, rhs) if rhs_transposed else jnp.dot(lhs, rhs, preferred_element_type=F32)
        return out[:, :128], out[:, 128:]

    v_ps = [v[:, ps].astype(BF16) for ps in pair_slices]
    s_at_ref = [s_prev[p] * e_ref[:, ps] for p, ps in enumerate(pair_slices)]
    inter, rhs = [None] * RWKV_PAIRS, [None] * RWKV_PAIRS
    for p in range(0, RWKV_PAIRS, 2):
        inter[p], inter[p + 1] = two_pairs(ar_rel[p], ar_rel[p + 1], s_at_ref[p], s_at_ref[p + 1],
                                           rhs_transposed=True)
        mv0, mv1 = two_pairs(jnp.concatenate([m_ak[2 * p], m_ak[2 * p + 1]], axis=0),
                             jnp.concatenate([m_ak[2 * p + 2], m_ak[2 * p + 3]], axis=0), v_ps[p], v_ps[p + 1])
        rhs[p] = inter[p][:L] + both_heads(mv0)
        rhs[p + 1] = inter[p + 1][:L] + both_heads(mv1)
    yield
    u_ps = [None] * RWKV_PAIRS
    for p in range(0, RWKV_PAIRS, 2):
        t0, t1 = t_inv[p], t_inv[p + 1]
        ts = [jnp.concatenate([t[:, :L], t[:, L:]], axis=0) for t in (t0, t1)]
        t_hi = [t.astype(BF16) for t in ts]
        t_lo = [(t - h.astype(F32)).astype(BF16) for t, h in zip(ts, t_hi)]
        r_hi = [rhs[p].astype(BF16), rhs[p + 1].astype(BF16)]
        r_lo = [(rhs[p + j] - r_hi[j].astype(F32)).astype(BF16) for j in range(2)]
        parts = [two_pairs(t_hi[0], t_hi[1], r_hi[0], r_hi[1]), two_pairs(t_lo[0], t_lo[1], r_hi[0], r_hi[1]),
                 two_pairs(t_hi[0], t_hi[1], r_lo[0], r_lo[1])]
        u_ps[p] = both_heads(parts[0][0] + parts[1][0] + parts[2][0])
        u_ps[p + 1] = both_heads(parts[0][1] + parts[1][1] + parts[2][1])
    uv = [jnp.concatenate([u_ps[p].astype(BF16), v_ps[p]], axis=0) for p in pairs]
    yield
    y_parts = [inter[p][L:] + both_heads(_bdot(jnp.concatenate([m_r[2 * p], m_r[2 * p + 1]], axis=0), uv[p]))
               for p in pairs]
    s_new = [s_prev[p] * e_end[:, ps] + (blockdiag * e_tail[:, ps]) * _bdot_tn(uv[p], bk_rel[p])
             for p, ps in enumerate(pair_slices)]
    yield

    y = jnp.concatenate(y_parts, axis=1)
    inv_n = 1.0 / RWKV_HEAD_DIM
    mean = _head_sums(y, seg) * inv_n
    yield 4
    yc = y - mean
    var = _head_sums(yc * yc, seg) * inv_n
    yield 4
    y = yc * lax.rsqrt(var + RWKV_LN_EPS) * prm["lnw"][...] + prm["lnb"][...]
    y = (y + bonus * v) * gate
    return m_outs, m_new, y, s_new


MIXER_PARAMS = ("conv", "gbias", "mnorm", "mu", "w0", "a0", "wa", "gup", "kk", "ka", "rk", "lnw", "lnb", "seg")


def _mixer_kernel(xn_ref, x0_ref, gin_ref, win_ref, *rest):
    prm = dict(zip(MIXER_PARAMS, rest[:len(MIXER_PARAMS)]))
    masks_ref, y_ref, znext, zvo, qkbuf, rwbuf, ct_ref, n_ref, m_ref, s_ref = rest[len(MIXER_PARAMS):]
    L = CHUNK
    t_idx = pl.program_id(1)

    @pl.when((t_idx == 0) & (pl.program_id(0) == 0))
    def _():
        u0 = _rms(x0_ref[...].reshape(STREAMS * L, D_MODEL), gin_ref[...]).astype(BF16)
        znext[...] = jnp.dot(u0, win_ref[...], preferred_element_type=F32)

    @pl.when(t_idx == 0)
    def _():
        qkbuf[...] = jnp.zeros_like(qkbuf)
        rwbuf[...] = jnp.zeros_like(rwbuf)
        ct_ref[...] = jnp.zeros_like(ct_ref)
        n_ref[...] = jnp.zeros_like(n_ref)
        m_ref[...] = jnp.zeros_like(m_ref)
        s_ref[...] = jnp.zeros_like(s_ref)

    @pl.when(t_idx > 0)
    def _():
        for q in range(STREAMS):
            qkbuf[q, 0:8, :] = qkbuf[q, L:L + 8, :]
            rwbuf[q, 0:8, :] = rwbuf[q, L:L + 8, :]

    for q in range(STREAMS):
        rows = slice(q * L, (q + 1) * L)
        qkbuf[q, 8:8 + L, :] = znext[rows, C_QK:C_QK + 2 * D_MLSTM]
        zvo[q] = znext[rows, C_MV:C_RW]
        rwbuf[q, 8:8 + L, :] = znext[rows, C_RW:C_END]

    streams = []
    for q in range(STREAMS):
        s_prev = [s_ref[q, p] for p in range(RWKV_PAIRS)]
        mlstm_states = [(ct_ref[q, h], n_ref[q, h], m_ref[q, h]) for h in range(MLSTM_HEADS)]
        streams.append(_chunk_stream(zvo.at[q], qkbuf.at[q], rwbuf.at[q], s_prev, mlstm_states, prm, masks_ref))

    results = _run(_interleaved(*streams, filler=_project_next(xn_ref, gin_ref, win_ref, znext)))

    for q, (m_outs, m_new, y, s_new) in enumerate(results):
        for h in range(MLSTM_HEADS):
            y_ref[q, :, h * MLSTM_HEAD_DIM:(h + 1) * MLSTM_HEAD_DIM] = m_outs[h].astype(y_ref.dtype)
            ct_ref[q, h], n_ref[q, h], m_ref[q, h] = m_new[h]
        y_ref[q, :, D_MLSTM:] = y.astype(y_ref.dtype)
        for p in range(RWKV_PAIRS):
            s_ref[q, p] = s_new[p]


def _mixer(x, gin, w_in, mixer_params, masks):
    b, t, d = x.shape
    nt = t // CHUNK
    const = lambda a: pl.BlockSpec(a.shape, lambda i, j: (0,) * a.ndim, pipeline_mode=pl.Buffered(1))
    params = (gin, w_in) + tuple(mixer_params) + (masks,)
    blk = (STREAMS, CHUNK, d)
    groups = b // STREAMS

    def next_chunk(i, j):
        wrap = j + 1 == nt
        return (jnp.where(wrap, jnp.minimum(i + 1, groups - 1), i), jnp.where(wrap, 0, j + 1), 0)

    return pl.pallas_call(
        _mixer_kernel,
        grid=(b // STREAMS, nt),
        in_specs=[pl.BlockSpec(blk, next_chunk),
                  pl.BlockSpec(blk, lambda i, j: (0, 0, 0), pipeline_mode=pl.Buffered(1))]
                 + [const(a) for a in params],
        out_specs=pl.BlockSpec(blk, lambda i, j: (i, j, 0)),
        out_shape=jax.ShapeDtypeStruct((b, t, d), BF16),
        scratch_shapes=[
            pltpu.VMEM((STREAMS * CHUNK, C_END), F32),
            pltpu.VMEM((STREAMS, CHUNK, C_RW - C_MV), F32),
            pltpu.VMEM((STREAMS, CHUNK + 8, 2 * D_MLSTM), F32),
            pltpu.VMEM((STREAMS, CHUNK + 8, D_RW_IN), F32),
            pltpu.VMEM((STREAMS, MLSTM_HEADS, MLSTM_HEAD_DIM, MLSTM_HEAD_DIM), F32),
            pltpu.VMEM((STREAMS, MLSTM_HEADS, 1, MLSTM_HEAD_DIM), F32),
            pltpu.VMEM((STREAMS, MLSTM_HEADS, 1, 128), F32),
            pltpu.VMEM((STREAMS, RWKV_PAIRS, 128, 128), F32),
        ],
        compiler_params=pltpu.CompilerParams(dimension_semantics=("arbitrary", "arbitrary"),
                                             vmem_limit_bytes=VMEM_LIMIT),
        name="mixer",
    )(x, x, *params)


def _tail_kernel(x_ref, y_ref, kv_ref, wmix_ref, gx_ref, wq_ref, wo_ref, gf_ref, wg_ref, wu_ref, wd_ref,
                 gfin_ref, o_ref):
    h = x_ref[...] + jnp.dot(y_ref[...], wmix_ref[...], preferred_element_type=F32)

    q = _bdot(_rms(h, gx_ref[...]), wq_ref[...]).astype(BF16)
    heads = []
    for i in range(XATTN_HEADS):
        hs = slice(i * XATTN_HEAD_DIM, (i + 1) * XATTN_HEAD_DIM)
        k_h = kv_ref[0, :, i * XATTN_HEAD_DIM:(i + 1) * XATTN_HEAD_DIM]
        v_h = kv_ref[0, :, D_MODEL + i * XATTN_HEAD_DIM:D_MODEL + (i + 1) * XATTN_HEAD_DIM]
        s = _bdot_nt(q[:, hs], k_h) * (XATTN_HEAD_DIM ** -0.5)
        e = jnp.exp(s - jnp.max(s, axis=-1, keepdims=True))
        heads.append((_bdot(e, v_h) / jnp.sum(e, axis=-1, keepdims=True)).astype(BF16))
    h = h + jnp.dot(jnp.concatenate(heads, axis=1), wo_ref[...], preferred_element_type=F32)

    u = _rms(h, gf_ref[...]).astype(BF16)
    for c0 in range(0, wg_ref.shape[1], FF_COLS):
        gate = jnp.dot(u, wg_ref[:, c0:c0 + FF_COLS], preferred_element_type=F32)
        up = jnp.dot(u, wu_ref[:, c0:c0 + FF_COLS], preferred_element_type=F32)
        act = (gate * _sigmoid(gate) * up).astype(BF16)
        h = h + jnp.dot(act, wd_ref[c0:c0 + FF_COLS, :], preferred_element_type=F32)

    o_ref[...] = _rms(h, gfin_ref[...])


def _tail(x2, y2, kv, tail_params, tm, t):
    n, d = x2.shape
    per_b = t // tm
    const = lambda a: pl.BlockSpec(a.shape, lambda i: (0,) * a.ndim, pipeline_mode=pl.Buffered(1))
    return pl.pallas_call(
        _tail_kernel,
        grid=(n // tm,),
        in_specs=[pl.BlockSpec((tm, d), lambda i: (i, 0)),
                  pl.BlockSpec((tm, d), lambda i: (i, 0)),
                  pl.BlockSpec((1,) + kv.shape[1:], lambda i: (i // per_b, 0, 0))]
                 + [const(a) for a in tail_params],
        out_specs=pl.BlockSpec((tm, d), lambda i: (i, 0)),
        out_shape=jax.ShapeDtypeStruct((n, d), F32),
        compiler_params=pltpu.CompilerParams(dimension_semantics=("parallel",),
                                             vmem_limit_bytes=VMEM_LIMIT),
        name="tail",
    )(x2, y2, kv, *tail_params)


def _chunk_masks():
    i = jnp.arange(CHUNK)
    row, col = i[:, None], i[None, :]
    planes = [None] * 8
    planes[MASK_LOWER] = row >= col
    planes[MASK_STRICT] = row > col
    for s, idx in MASK_SAME.items():
        planes[idx] = (row // s) == (col // s)
    for s, idx in MASK_OFF.items():
        planes[idx] = ((row // (2 * s)) == (col // (2 * s))) & ((row // s) > (col // s))
    return jnp.stack(planes).astype(BF16)


def _regroup_w_in(w):
    n_m = 4 * D_MLSTM
    gates = w[:, n_m:n_m + 2 * MLSTM_HEADS]
    gates = jnp.pad(gates, ((0, 0), (0, 128 - 2 * MLSTM_HEADS)))
    return jnp.concatenate([w[:, :n_m], gates, w[:, n_m + 2 * MLSTM_HEADS:]], axis=1).astype(BF16)


def kernel(x, mem, norm_mix, w_in, mlstm_conv, mlstm_i_bias, mlstm_f_bias, mlstm_norm, rwkv_mu, rwkv_w0, rwkv_w_up, rwkv_a0, rwkv_a_up, rwkv_g_up, rwkv_k_k, rwkv_k_a, rwkv_r_k, rwkv_ln_w, rwkv_ln_b, w_mix_out, norm_xattn, norm_mem, xattn_wq, xattn_wkv, xattn_wo, norm_ffn, ffn_w_gate, ffn_w_up, ffn_w_down, norm_final):
    b, t, d = x.shape
    assert d == D_MODEL and t % CHUNK == 0 and b % STREAMS == 0 and norm_mix.shape[0] == 1
    row = lambda a: a.reshape(1, -1).astype(F32)

    kv = _mem_kv(mem, row(norm_mem[0]), xattn_wkv[0].astype(BF16))

    gbias = jnp.pad(jnp.concatenate([mlstm_i_bias[0], mlstm_f_bias[0]]), (0, 128 - 2 * MLSTM_HEADS))
    zeros = jnp.zeros((DECAY_LORA, D_RWKV), F32)
    wa = jnp.concatenate([jnp.concatenate([rwkv_w_up[0], zeros], axis=1),
                          jnp.concatenate([zeros, rwkv_a_up[0]], axis=1)], axis=0).astype(BF16)
    head_id = jnp.arange(D_RWKV // 2) // RWKV_HEAD_DIM
    seg = (head_id[:, None] == head_id[None, :]).astype(BF16)
    mixer_params = (mlstm_conv[0], row(gbias), row(mlstm_norm[0]), row(rwkv_mu[0]), row(rwkv_w0[0]),
                    row(rwkv_a0[0]), wa, rwkv_g_up[0].astype(BF16), row(rwkv_k_k[0]), row(rwkv_k_a[0]),
                    row(rwkv_r_k[0]), row(rwkv_ln_w[0]), row(rwkv_ln_b[0]), seg)
    tail_params = (w_mix_out[0].astype(BF16), row(norm_xattn[0]), xattn_wq[0].astype(BF16),
                   xattn_wo[0].astype(BF16), row(norm_ffn[0]), ffn_w_gate[0].astype(BF16),
                   ffn_w_up[0].astype(BF16), ffn_w_down[0].astype(BF16), row(norm_final))
    y = _mixer(x, row(norm_mix[0]), _regroup_w_in(w_in[0]), mixer_params, _chunk_masks())
    tm = TAIL_ROWS if t % TAIL_ROWS == 0 else CHUNK
    out = _tail(x.reshape(b * t, d), y.reshape(b * t, d), kv, tail_params, tm, t)
    return out.reshape(b, t, d)
```
